```python
import jax, jax.numpy as jnp
from jax import lax
import numpy as np

D_MODEL = 2048
BATCH = 8
SEQ = 2048
DEPTH = 2

D_MIX = D_MODEL
CONV_CH = D_MIX // 4
CONV_WIDTH = 31
MLSTM_HEADS = 4
MLSTM_DH = D_MIX // 8
MLSTM_CH = MLSTM_HEADS * MLSTM_DH
QK_CONV_WIDTH = 4
MLSTM_CHUNK = 128
GM_GROUPS = 4
GM_CH = D_MIX // 4
GM_GROUP_CH = GM_CH // GM_GROUPS
GM_CHUNK = 128
D_FF = 4 * D_MODEL
EPS = 1e-6

IN_SIZES = (CONV_CH, CONV_CH,
            MLSTM_CH, MLSTM_CH, MLSTM_CH, MLSTM_CH,
            MLSTM_HEADS, MLSTM_HEADS,
            GM_CH, GM_CH)
N_IN = sum(IN_SIZES)
IN_SPLITS = tuple(int(s) for s in np.cumsum(IN_SIZES)[:-1])

kernel_name = "hybrid_conv_mlstm_gmlp_trunk"


def rmsnorm(x, g):
    xf = x.astype(jnp.float32)
    y = xf * lax.rsqrt(jnp.mean(xf * xf, axis=-1, keepdims=True) + EPS)
    return (y * g.astype(jnp.float32)).astype(x.dtype)


def layernorm(x, g, b):
    xf = x.astype(jnp.float32)
    mu = jnp.mean(xf, axis=-1, keepdims=True)
    var = jnp.mean(jnp.square(xf - mu), axis=-1, keepdims=True)
    y = (xf - mu) * lax.rsqrt(var + EPS)
    return (y * g.astype(jnp.float32) + b.astype(jnp.float32)).astype(x.dtype)


def causal_dwconv(x, w, b):
    K, C = w.shape
    y = lax.conv_general_dilated(
        x, w[:, None, :].astype(x.dtype), window_strides=(1,), padding=[(K - 1, 0)],
        dimension_numbers=("NWC", "WIO", "NWC"), feature_group_count=C)
    return y + b.astype(x.dtype)


def mlstm_chunkwise(q, k, v, li, lf):
    B, S, H, D = q.shape
    L = MLSTM_CHUNK
    NC = S // L
    f32 = jnp.float32

    def to_chunks(t):
        return t.astype(f32).reshape(B, NC, L, H, D).transpose(1, 0, 3, 2, 4)

    def gate_chunks(t):
        return t.astype(f32).reshape(B, NC, L, H).transpose(1, 0, 3, 2)

    causal = jnp.tril(jnp.ones((L, L), dtype=bool))

    def step(carry, xs):
        C, n, m = carry
        qc, kc, vc, lic, lfc = xs
        b = jnp.cumsum(lfc, axis=-1)
        dmat = jnp.where(causal, b[..., :, None] - b[..., None, :] + lic[..., None, :], -jnp.inf)
        inter = b + m[..., None]
        m_row = jnp.maximum(jnp.max(dmat, axis=-1), inter)
        p = jnp.exp(dmat - m_row[..., None]) * jnp.einsum("bhld,bhsd->bhls", qc, kc)
        g = jnp.exp(inter - m_row)
        num = jnp.einsum("bhls,bhse->bhle", p, vc) + g[..., None] * jnp.einsum("bhld,bhde->bhle", qc, C)
        den = jnp.sum(p, axis=-1) + g * jnp.einsum("bhld,bhd->bhl", qc, n)
        h = num / jnp.maximum(jnp.abs(den), jnp.exp(-m_row))[..., None]
        bL = b[..., -1]
        a = bL[..., None] - b + lic
        m_new = jnp.maximum(bL + m, jnp.max(a, axis=-1))
        wk = jnp.exp(a - m_new[..., None])
        decay = jnp.exp(bL + m - m_new)
        kw = kc * wk[..., None]
        C_new = decay[..., None, None] * C + jnp.einsum("bhld,bhle->bhde", kw, vc)
        n_new = decay[..., None] * n + jnp.sum(kw, axis=2)
        return (C_new, n_new, m_new), h

    init = (jnp.zeros((B, H, D, D), f32), jnp.zeros((B, H, D), f32), jnp.zeros((B, H), f32))
    _, hs = lax.scan(step, init, (to_chunks(q), to_chunks(k), to_chunks(v), gate_chunks(li), gate_chunks(lf)))
    return hs.transpose(1, 0, 3, 2, 4).reshape(B, S, H, D).astype(q.dtype)


def hybrid_mixer(x, ln_g, w_in, conv_w, conv_b, conv_norm_g, conv_norm_b, qk_conv_w, qk_conv_b,
                 igate_b, fgate_b, mlstm_norm_g, gm_norm_g, gm_norm_b, gm_w, gm_b, w_out):
    B, S, _ = x.shape
    h = rmsnorm(x, ln_g)
    z = h @ w_in
    cv, cg, q, k, v, o, ig, fg, gu, gv = jnp.split(z, IN_SPLITS, axis=-1)

    a = cv * jax.nn.sigmoid(cg)
    a = causal_dwconv(a, conv_w, conv_b)
    a = jax.nn.silu(layernorm(a, conv_norm_g, conv_norm_b))

    qk = jax.nn.silu(causal_dwconv(jnp.concatenate([q, k], axis=-1), qk_conv_w, qk_conv_b))
    q, k = jnp.split(qk, 2, axis=-1)
    heads = lambda t: t.reshape(B, S, MLSTM_HEADS, MLSTM_DH)
    li = ig.astype(jnp.float32) + igate_b.astype(jnp.float32)
    lf = jax.nn.log_sigmoid(fg.astype(jnp.float32) + fgate_b.astype(jnp.float32))
    hb = mlstm_chunkwise(heads(q), heads(k) * (MLSTM_DH ** -0.5), heads(v), li, lf)
    hbf = hb.astype(jnp.float32)
    hbf = hbf * lax.rsqrt(jnp.mean(hbf * hbf, axis=-1, keepdims=True) + EPS)
    hb = (hbf.reshape(B, S, MLSTM_CH) * mlstm_norm_g.astype(jnp.float32)).astype(x.dtype)
    hb = hb * jax.nn.sigmoid(o)

    gu = jax.nn.gelu(gu)
    gv = layernorm(jax.nn.gelu(gv), gm_norm_g, gm_norm_b)
    nch = S // GM_CHUNK
    gvr = gv.reshape(B, nch, GM_CHUNK, GM_GROUPS, GM_GROUP_CH)
    w_causal = gm_w * jnp.tril(jnp.ones((GM_CHUNK, GM_CHUNK), gm_w.dtype))
    sp = jnp.einsum("gts,bcsge->bctge", w_causal, gvr) + gm_b.T[None, None, :, :, None]
    c = gu * sp.reshape(B, S, GM_CH)

    return jnp.concatenate([a, hb, c], axis=-1) @ w_out


def setup_inputs(seed: int = 0) -> dict:
    key = jax.random.key(seed)
    ks = jax.random.split(key, 24)
    f32 = jnp.float32
    nrm = lambda k, shape, scale: jax.random.normal(k, shape, f32) * scale
    fbias = jnp.linspace(3.0, 6.0, MLSTM_HEADS, dtype=f32)[None, :] + nrm(ks[9], (DEPTH, MLSTM_HEADS), 0.1)
    return {
        "x": nrm(ks[0], (BATCH, SEQ, D_MODEL), 1.0),
        "ln_mix_g": 1.0 + nrm(ks[1], (DEPTH, D_MODEL), 0.02),
        "w_in": nrm(ks[2], (DEPTH, D_MODEL, N_IN), D_MODEL ** -0.5),
        "conv_w": nrm(ks[3], (DEPTH, CONV_WIDTH, CONV_CH), CONV_WIDTH ** -0.5),
        "conv_b": nrm(ks[4], (DEPTH, CONV_CH), 0.02),
        "conv_norm_g": 1.0 + nrm(ks[5], (DEPTH, CONV_CH), 0.02),
        "conv_norm_b": nrm(ks[6], (DEPTH, CONV_CH), 0.02),
        "qk_conv_w": nrm(ks[7], (DEPTH, QK_CONV_WIDTH, 2 * MLSTM_CH), QK_CONV_WIDTH ** -0.5),
        "qk_conv_b": nrm(ks[8], (DEPTH, 2 * MLSTM_CH), 0.02),
        "igate_b": nrm(ks[10], (DEPTH, MLSTM_HEADS), 0.1),
        "fgate_b": fbias,
        "mlstm_norm_g": 1.0 + nrm(ks[11], (DEPTH, MLSTM_CH), 0.02),
        "gm_norm_g": 1.0 + nrm(ks[12], (DEPTH, GM_CH), 0.02),
        "gm_norm_b": nrm(ks[13], (DEPTH, GM_CH), 0.02),
        "gm_w": nrm(ks[14], (DEPTH, GM_GROUPS, GM_CHUNK, GM_CHUNK), GM_CHUNK ** -0.5),
        "gm_b": 1.0 + nrm(ks[15], (DEPTH, GM_GROUPS, GM_CHUNK), 0.02),
        "w_out": nrm(ks[16], (DEPTH, D_MIX, D_MODEL), D_MIX ** -0.5),
        "ln_mlp_g": 1.0 + nrm(ks[17], (DEPTH, D_MODEL), 0.02),
        "w_up": nrm(ks[18], (DEPTH, D_MODEL, D_FF), D_MODEL ** -0.5),
        "w_down": nrm(ks[19], (DEPTH, D_FF, D_MODEL), D_FF ** -0.5),
        "final_g": 1.0 + nrm(ks[20], (D_MODEL,), 0.02),
    }


def reference(x, ln_mix_g, w_in, conv_w, conv_b, conv_norm_g, conv_norm_b, qk_conv_w, qk_conv_b,
              igate_b, fgate_b, mlstm_norm_g, gm_norm_g, gm_norm_b, gm_w, gm_b, w_out,
              ln_mlp_g, w_up, w_down, final_g):
    for l in range(DEPTH):
        x = x + hybrid_mixer(x, ln_mix_g[l], w_in[l], conv_w[l], conv_b[l], conv_norm_g[l], conv_norm_b[l],
                             qk_conv_w[l], qk_conv_b[l], igate_b[l], fgate_b[l], mlstm_norm_g[l],
                             gm_norm_g[l], gm_norm_b[l], gm_w[l], gm_b[l], w_out[l])
        hm = rmsnorm(x, ln_mlp_g[l]) @ w_up[l]
        x = x + jnp.square(jax.nn.relu(hm)) @ w_down[l]
    return rmsnorm(x, final_g)
```

```python
import functools

import jax
import jax.numpy as jnp
from jax import lax
from jax.experimental import pallas as pl
from jax.experimental.pallas import tpu as pltpu

F32 = jnp.float32
BF16 = jnp.bfloat16

EPS = 1e-6
CONV_CH = 512
CONV_WIDTH = 31
HEADS = 4
DH = 256
MLSTM_CH = HEADS * DH
QK_WIDTH = 4
CHUNK = 128
GM_GROUPS = 4
GM_CH = 512
GM_GROUP_CH = GM_CH // GM_GROUPS
LANES = 128
SUBLANES = 8

Z_CV = 0
Z_CG = Z_CV + CONV_CH
Z_Q = Z_CG + CONV_CH
Z_K = Z_Q + MLSTM_CH
Z_V = Z_K + MLSTM_CH
Z_O = Z_V + MLSTM_CH
Z_GU = Z_O + MLSTM_CH
Z_GV = Z_GU + GM_CH
Z_COLS = Z_GV + GM_CH
MIX_A = 0
MIX_B = CONV_CH
MIX_C = CONV_CH + MLSTM_CH

CONV_HALO = 32
QK_HALO = 8

VMEM_LIMIT = 56 * 1024 * 1024


def _rms(x, g):
    return x * lax.rsqrt(jnp.mean(x * x, axis=-1, keepdims=True) + EPS) * g


def _layernorm(x, g, b):
    mu = jnp.mean(x, axis=-1, keepdims=True)
    xc = x - mu
    var = jnp.mean(xc * xc, axis=-1, keepdims=True)
    return xc * lax.rsqrt(var + EPS) * g + b


def _in_proj_kernel(x_ref, g_ref, w_ref, wg_ref, z_ref, gate_ref, xn_ref):
    @pl.when(pl.program_id(1) == 0)
    def _():
        xn = _rms(x_ref[...], g_ref[...]).astype(BF16)
        xn_ref[...] = xn
        gate_ref[...] = jnp.dot(xn, wg_ref[...], preferred_element_type=F32)

    z_ref[...] = jnp.dot(xn_ref[...], w_ref[...], preferred_element_type=F32)


def _in_proj(x2, g, w, wg, *, tm=1024, tn=512):
    m, d = x2.shape
    n = w.shape[1]
    return pl.pallas_call(
        _in_proj_kernel,
        grid=(m // tm, n // tn),
        in_specs=[
            pl.BlockSpec((tm, d), lambda i, j: (i, 0)),
            pl.BlockSpec((1, d), lambda i, j: (0, 0)),
            pl.BlockSpec((d, tn), lambda i, j: (0, j)),
            pl.BlockSpec((d, LANES), lambda i, j: (0, 0)),
        ],
        out_specs=[
            pl.BlockSpec((tm, tn), lambda i, j: (i, j)),
            pl.BlockSpec((tm, LANES), lambda i, j: (i, 0)),
        ],
        out_shape=[
            jax.ShapeDtypeStruct((m, n), F32),
            jax.ShapeDtypeStruct((m, LANES), F32),
        ],
        scratch_shapes=[pltpu.VMEM((tm, d), BF16)],
        compiler_params=pltpu.CompilerParams(
            dimension_semantics=("arbitrary", "arbitrary"),
            vmem_limit_bytes=VMEM_LIMIT),
        name="in_proj",
    )(x2, g, w, wg)


def _cumsum_rows(x):
    rows = lax.broadcasted_iota(jnp.int32, x.shape, 0)
    shift = 1
    while shift < x.shape[0]:
        x = x + jnp.where(rows >= shift, pltpu.roll(x, shift, axis=0), 0.0)
        shift *= 2
    return x


def _mixer_kernel(z_ref, gate_ref, x_ref, convw_ref, convb_ref, cng_ref, cnb_ref,
                  qkw_ref, qkb_ref, gbias_ref, mng_ref, gng_ref, gnb_ref, gmw_ref,
                  gmbt_ref, wout_ref, o_ref,
                  abuf, qkbuf, c_ref, n_ref, m_ref, mix_ref):
    L = CHUNK

    @pl.when(pl.program_id(1) == 0)
    def _():
        abuf[0:CONV_HALO, :] = jnp.zeros((CONV_HALO, CONV_CH), F32)
        qkbuf[0:QK_HALO, :] = jnp.zeros((QK_HALO, 2 * MLSTM_CH), F32)
        c_ref[...] = jnp.zeros_like(c_ref)
        n_ref[...] = jnp.zeros_like(n_ref)
        m_ref[...] = jnp.zeros_like(m_ref)

    cv = z_ref[:, Z_CV:Z_CV + CONV_CH]
    cg = z_ref[:, Z_CG:Z_CG + CONV_CH]
    abuf[CONV_HALO:CONV_HALO + L, :] = cv * jax.nn.sigmoid(cg)
    acc = jnp.broadcast_to(convb_ref[...], (L, CONV_CH))
    base = CONV_HALO - (CONV_WIDTH - 1)
    for k in range(CONV_WIDTH):
        acc = acc + convw_ref[k:k + 1, :] * abuf[base + k:base + k + L, :]
    abuf[0:CONV_HALO, :] = abuf[L:L + CONV_HALO, :]
    a = _layernorm(acc, cng_ref[...], cnb_ref[...])
    mix_ref[:, MIX_A:MIX_A + CONV_CH] = (a * jax.nn.sigmoid(a)).astype(BF16)

    gu = jax.nn.gelu(z_ref[:, Z_GU:Z_GU + GM_CH])
    gv = _layernorm(jax.nn.gelu(z_ref[:, Z_GV:Z_GV + GM_CH]), gng_ref[...], gnb_ref[...])
    rows = lax.broadcasted_iota(jnp.int32, (L, L), 0)
    cols = lax.broadcasted_iota(jnp.int32, (L, L), 1)
    causal = cols <= rows
    for g in range(GM_GROUPS):
        sl = slice(g * GM_GROUP_CH, (g + 1) * GM_GROUP_CH)
        wc = jnp.where(causal, gmw_ref[g], 0.0).astype(BF16)
        sp = jnp.dot(wc, gv[:, sl].astype(BF16), preferred_element_type=F32)
        sp = sp + gmbt_ref[:, g:g + 1]
        mix_ref[:, MIX_C + g * GM_GROUP_CH:MIX_C + (g + 1) * GM_GROUP_CH] = (gu[:, sl] * sp).astype(BF16)

    pre = gate_ref[...] + gbias_ref[...]
    lanes = lax.broadcasted_iota(jnp.int32, (L, LANES), 1)
    gl = jnp.where(lanes < HEADS, pre, jax.nn.log_sigmoid(pre))
    bc = _cumsum_rows(gl)
    gl_t = gl.T
    bc_t = bc.T

    qkbuf[QK_HALO:QK_HALO + L, :] = z_ref[:, Z_Q:Z_Q + 2 * MLSTM_CH]
    qk_base = QK_HALO - (QK_WIDTH - 1)

    def short_conv(col):
        y = jnp.broadcast_to(qkb_ref[:, col:col + DH], (L, DH))
        for k in range(QK_WIDTH):
            y = y + qkw_ref[k:k + 1, col:col + DH] * qkbuf[qk_base + k:qk_base + k + L, col:col + DH]
        return y * jax.nn.sigmoid(y)

    for h in range(HEADS):
        q_h = short_conv(h * DH)
        k_h = short_conv(MLSTM_CH + h * DH) * (DH ** -0.5)
        v_b = z_ref[:, Z_V + h * DH:Z_V + (h + 1) * DH].astype(BF16)
        o_h = z_ref[:, Z_O + h * DH:Z_O + (h + 1) * DH]
        li_col = gl[:, h:h + 1]
        b_col = bc[:, HEADS + h:HEADS + h + 1]
        li_row = gl_t[h:h + 1, :]
        b_row = bc_t[HEADS + h:HEADS + h + 1, :]
        m_prev = m_ref[h][:, 0:1]

        dmat = jnp.where(causal, b_col - b_row + li_row, -jnp.inf)
        inter = b_col + m_prev
        m_row = jnp.maximum(jnp.max(dmat, axis=-1, keepdims=True), inter)
        q_b = q_h.astype(BF16)
        s = lax.dot_general(q_b, k_h.astype(BF16), (((1,), (1,)), ((), ())),
                            preferred_element_type=F32)
        p = jnp.exp(dmat - m_row) * s
        gdec = jnp.exp(inter - m_row)
        c_old = c_ref[h]
        n_old = n_ref[h]
        num = (jnp.dot(p.astype(BF16), v_b, preferred_element_type=F32)
               + gdec * jnp.dot(q_b, c_old.astype(BF16), preferred_element_type=F32))
        den = (jnp.sum(p, axis=-1, keepdims=True)
               + gdec * jnp.sum(q_h * n_old, axis=-1, keepdims=True))
        hh = num / jnp.maximum(jnp.abs(den), jnp.exp(-m_row))

        b_last = b_col[L - 1:L, :]
        a_col = b_last - b_col + li_col
        m_new = jnp.maximum(b_last + m_prev, jnp.max(a_col, axis=0, keepdims=True))
        decay = jnp.exp(b_last + m_prev - m_new)
        kw = k_h * jnp.exp(a_col - m_new)
        c_ref[h] = decay * c_old + jnp.dot(kw.T.astype(BF16), v_b, preferred_element_type=F32)
        n_ref[h] = decay * n_old + jnp.sum(kw, axis=0, keepdims=True)
        m_ref[h] = jnp.broadcast_to(m_new, (1, LANES))

        hn = _rms(hh, mng_ref[:, h * DH:(h + 1) * DH])
        mix_ref[:, MIX_B + h * DH:MIX_B + (h + 1) * DH] = (hn * jax.nn.sigmoid(o_h)).astype(BF16)

    qkbuf[0:QK_HALO, :] = qkbuf[L:L + QK_HALO, :]

    o_ref[...] = x_ref[...] + jnp.dot(mix_ref[...], wout_ref[...], preferred_element_type=F32)


def _mixer(z, gates, x2, p, *, batch, seq):
    m, d = x2.shape
    nc = seq // CHUNK
    row = lambda b, c: (b * nc + c, 0)
    const2 = lambda b, c: (0, 0)
    full = lambda a: pl.BlockSpec(a.shape, (lambda b, c: (0,) * a.ndim))
    small = [p["conv_w"], p["conv_b"], p["conv_norm_g"], p["conv_norm_b"], p["qk_conv_w"],
             p["qk_conv_b"], p["gate_bias"], p["mlstm_norm_g"], p["gm_norm_g"], p["gm_norm_b"],
             p["gm_w"], p["gm_bt"]]
    return pl.pallas_call(
        _mixer_kernel,
        grid=(batch, nc),
        in_specs=[
            pl.BlockSpec((CHUNK, Z_COLS), row),
            pl.BlockSpec((CHUNK, LANES), row),
            pl.BlockSpec((CHUNK, d), row),
            *[full(a) for a in small],
            pl.BlockSpec(p["w_out"].shape, const2),
        ],
        out_specs=pl.BlockSpec((CHUNK, d), row),
        out_shape=jax.ShapeDtypeStruct((m, d), F32),
        scratch_shapes=[
            pltpu.VMEM((CONV_HALO + CHUNK, CONV_CH), F32),
            pltpu.VMEM((QK_HALO + CHUNK, 2 * MLSTM_CH), F32),
            pltpu.VMEM((HEADS, DH, DH), F32),
            pltpu.VMEM((HEADS, 1, DH), F32),
            pltpu.VMEM((HEADS, 1, LANES), F32),
            pltpu.VMEM((CHUNK, d), BF16),
        ],
        compiler_params=pltpu.CompilerParams(
            dimension_semantics=("arbitrary", "arbitrary"),
            vmem_limit_bytes=VMEM_LIMIT),
        name="mixer",
    )(z, gates, x2, *small, p["w_out"])


def _mlp_kernel(x_ref, g_ref, wu_ref, wd_ref, fg_ref, o_ref, xn_ref, *, final_norm):
    k = pl.program_id(1)

    @pl.when(k == 0)
    def _():
        x = x_ref[...]
        xn_ref[...] = _rms(x, g_ref[...]).astype(BF16)
        o_ref[...] = x

    hm = jnp.dot(xn_ref[...], wu_ref[...], preferred_element_type=F32)
    hm = jnp.maximum(hm, 0.0)
    o_ref[...] += jnp.dot((hm * hm).astype(BF16), wd_ref[...], preferred_element_type=F32)

    if final_norm:
        @pl.when(k == pl.num_programs(1) - 1)
        def _():
            o_ref[...] = _rms(o_ref[...], fg_ref[...])


def _mlp(x2, g, wu, wd, fg, *, final_norm, tm=1024, tf=512):
    m, d = x2.shape
    f = wu.shape[1]
    return pl.pallas_call(
        functools.partial(_mlp_kernel, final_norm=final_norm),
        grid=(m // tm, f // tf),
        in_specs=[
            pl.BlockSpec((tm, d), lambda i, k: (i, 0)),
            pl.BlockSpec((1, d), lambda i, k: (0, 0)),
            pl.BlockSpec((d, tf), lambda i, k: (0, k)),
            pl.BlockSpec((tf, d), lambda i, k: (k, 0)),
            pl.BlockSpec((1, d), lambda i, k: (0, 0)),
        ],
        out_specs=pl.BlockSpec((tm, d), lambda i, k: (i, 0)),
        out_shape=jax.ShapeDtypeStruct((m, d), F32),
        scratch_shapes=[pltpu.VMEM((tm, d), BF16)],
        compiler_params=pltpu.CompilerParams(
            dimension_semantics=("arbitrary", "arbitrary"),
            vmem_limit_bytes=VMEM_LIMIT),
        name="mlp",
    )(x2, g, wu, wd, fg)


def _prep_w_in(w):
    cv, cg, q, k, v, o, ig, fg, gu, gv = jnp.split(
        w, [512, 1024, 2048, 3072, 4096, 5120, 5124, 5128, 5640], axis=-1)
    main = jnp.concatenate([cv, cg, q, k, v, o, gu, gv], axis=-1).astype(BF16)
    gates = jnp.concatenate([ig, fg], axis=-1)
    gates = jnp.pad(gates, ((0, 0), (0, LANES - 2 * HEADS))).astype(BF16)
    return main, gates


def kernel(x, ln_mix_g, w_in, conv_w, conv_b, conv_norm_g, conv_norm_b, qk_conv_w, qk_conv_b,
           igate_b, fgate_b, mlstm_norm_g, gm_norm_g, gm_norm_b, gm_w, gm_b, w_out,
           ln_mlp_g, w_up, w_down, final_g):
    batch, seq, d = x.shape
    depth = w_in.shape[0]
    x2 = x.reshape(batch * seq, d)
    row = lambda a: a.reshape(1, -1)
    for l in range(depth):
        w_main, w_gate = _prep_w_in(w_in[l])
        z, gates = _in_proj(x2, row(ln_mix_g[l]), w_main, w_gate)
        gate_bias = jnp.pad(jnp.concatenate([igate_b[l], fgate_b[l]]), (0, LANES - 2 * HEADS))
        params = dict(
            conv_w=conv_w[l], conv_b=row(conv_b[l]), conv_norm_g=row(conv_norm_g[l]),
            conv_norm_b=row(conv_norm_b[l]), qk_conv_w=qk_conv_w[l], qk_conv_b=row(qk_conv_b[l]),
            gate_bias=row(gate_bias), mlstm_norm_g=row(mlstm_norm_g[l]),
            gm_norm_g=row(gm_norm_g[l]), gm_norm_b=row(gm_norm_b[l]), gm_w=gm_w[l],
            gm_bt=gm_b[l].T, w_out=w_out[l].astype(BF16))
        x2 = _mixer(z, gates, x2, params, batch=batch, seq=seq)
        x2 = _mlp(x2, row(ln_mlp_g[l]), w_up[l].astype(BF16), w_down[l].astype(BF16),
                  row(final_g), final_norm=(l == depth - 1))
    return x2.reshape(batch, seq, d)
```

```python
import functools

import jax
import jax.numpy as jnp
from jax import lax
from jax.experimental import pallas as pl
from jax.experimental.pallas import tpu as pltpu

F32 = jnp.float32
BF16 = jnp.bfloat16

EPS = 1e-6
CONV_CH = 512
CONV_WIDTH = 31
HEADS = 4
DH = 256
MLSTM_CH = HEADS * DH
QK_WIDTH = 4
CHUNK = 128
GM_GROUPS = 4
GM_CH = 512
GM_GROUP_CH = GM_CH // GM_GROUPS
LANES = 128
SUBLANES = 8

Z_CV = 0
Z_CG = Z_CV + CONV_CH
Z_Q = Z_CG + CONV_CH
Z_K = Z_Q + MLSTM_CH
Z_V = Z_K + MLSTM_CH
Z_O = Z_V + MLSTM_CH
Z_GU = Z_O + MLSTM_CH
Z_GV = Z_GU + GM_CH
Z_COLS = Z_GV + GM_CH
MIX_A = 0
MIX_B = CONV_CH
MIX_C = CONV_CH + MLSTM_CH

CONV_HALO = 32
QK_HALO = 8

VMEM_LIMIT = 56 * 1024 * 1024


def _rms(x, g):
    return x * lax.rsqrt(jnp.mean(x * x, axis=-1, keepdims=True) + EPS) * g


def _layernorm(x, g, b):
    mu = jnp.mean(x, axis=-1, keepdims=True)
    xc = x - mu
    var = jnp.mean(xc * xc, axis=-1, keepdims=True)
    return xc * lax.rsqrt(var + EPS) * g + b


def _in_proj_kernel(x_ref, g_ref, w_ref, wg_ref, z_ref, gate_ref, xn_ref):
    @pl.when(pl.program_id(1) == 0)
    def _():
        xn = _rms(x_ref[...], g_ref[...]).astype(BF16)
        xn_ref[...] = xn
        gate_ref[...] = jnp.dot(xn, wg_ref[...], preferred_element_type=F32)

    z_ref[...] = jnp.dot(xn_ref[...], w_ref[...], preferred_element_type=F32)


def _in_proj(x2, g, w, wg, *, tm=1024, tn=512):
    m, d = x2.shape
    n = w.shape[1]
    return pl.pallas_call(
        _in_proj_kernel,
        grid=(m // tm, n // tn),
        in_specs=[
            pl.BlockSpec((tm, d), lambda i, j: (i, 0)),
            pl.BlockSpec((1, d), lambda i, j: (0, 0)),
            pl.BlockSpec((d, tn), lambda i, j: (0, j)),
            pl.BlockSpec((d, LANES), lambda i, j: (0, 0)),
        ],
        out_specs=[
            pl.BlockSpec((tm, tn), lambda i, j: (i, j)),
            pl.BlockSpec((tm, LANES), lambda i, j: (i, 0)),
        ],
        out_shape=[
            jax.ShapeDtypeStruct((m, n), F32),
            jax.ShapeDtypeStruct((m, LANES), F32),
        ],
        scratch_shapes=[pltpu.VMEM((tm, d), BF16)],
        compiler_params=pltpu.CompilerParams(
            dimension_semantics=("arbitrary", "arbitrary"),
            vmem_limit_bytes=VMEM_LIMIT),
        name="in_proj",
    )(x2, g, w, wg)


def _sigmoid(x):
    return 0.5 * jnp.tanh(0.5 * x) + 0.5


def _cumsum_rows(x):
    rows = lax.broadcasted_iota(jnp.int32, x.shape, 0)
    shift = 1
    while shift < x.shape[0]:
        x = x + jnp.where(rows >= shift, pltpu.roll(x, shift, axis=0), 0.0)
        shift *= 2
    return x


def _shift_up(x, r):
    return x if r == 0 else pltpu.roll(x, x.shape[0] - r, axis=0)


def _causal_conv(buf, w_ref, b_ref, cols, *, width, halo):
    L = CHUNK
    x = buf[:, cols]
    acc = jnp.broadcast_to(b_ref[:, cols], (L, LANES))
    first = halo - (width - 1)
    for r in range(SUBLANES):
        starts = [s for s in range(first, halo + 1) if s % SUBLANES == r]
        if not starts:
            continue
        xs = _shift_up(x, r)
        for s in starts:
            k = s - first
            a0 = s - r
            acc = acc + w_ref[k:k + 1, cols] * xs[a0:a0 + L]
    return acc


def _mixer_kernel(z_ref, gate_ref, x_ref, convw_ref, convb_ref, cng_ref, cnb_ref,
                  qkw_ref, qkb_ref, gbias_ref, mng_ref, gng_ref, gnb_ref, gmw_ref,
                  gmbt_ref, wout_ref, o_ref,
                  abuf, qkbuf, c_ref, n_ref, m_ref, mix_ref, *, chunks):
    L = CHUNK

    @pl.when(pl.program_id(1) == 0)
    def _():
        abuf[0:CONV_HALO, :] = jnp.zeros((CONV_HALO, CONV_CH), F32)
        qkbuf[0:QK_HALO, :] = jnp.zeros((QK_HALO, 2 * MLSTM_CH), F32)
        c_ref[...] = jnp.zeros_like(c_ref)
        n_ref[...] = jnp.zeros_like(n_ref)
        m_ref[...] = jnp.zeros_like(m_ref)

    rows = lax.broadcasted_iota(jnp.int32, (L, L), 0)
    cols = lax.broadcasted_iota(jnp.int32, (L, L), 1)
    causal = cols <= rows
    lanes = lax.broadcasted_iota(jnp.int32, (L, LANES), 1)

    def chunk_body(ci, carry):
        r0 = ci * L
        zc = lambda c0, n: z_ref[pl.ds(r0, L), c0:c0 + n]

        abuf[CONV_HALO:CONV_HALO + L, :] = zc(Z_CV, CONV_CH) * _sigmoid(zc(Z_CG, CONV_CH))
        acc = jnp.concatenate(
            [_causal_conv(abuf, convw_ref, convb_ref, slice(c * LANES, (c + 1) * LANES),
                          width=CONV_WIDTH, halo=CONV_HALO) for c in range(CONV_CH // LANES)],
            axis=-1)
        abuf[0:CONV_HALO, :] = abuf[L:L + CONV_HALO, :]
        a = _layernorm(acc, cng_ref[...], cnb_ref[...])
        mix_ref[pl.ds(r0, L), MIX_A:MIX_A + CONV_CH] = (a * _sigmoid(a)).astype(BF16)

        gu = jax.nn.gelu(zc(Z_GU, GM_CH))
        gv = _layernorm(jax.nn.gelu(zc(Z_GV, GM_CH)), gng_ref[...], gnb_ref[...])
        for g in range(GM_GROUPS):
            sl = slice(g * GM_GROUP_CH, (g + 1) * GM_GROUP_CH)
            wc = jnp.where(causal, gmw_ref[g], 0.0).astype(BF16)
            sp = jnp.dot(wc, gv[:, sl].astype(BF16), preferred_element_type=F32)
            sp = sp + gmbt_ref[:, g:g + 1]
            c0 = MIX_C + g * GM_GROUP_CH
            mix_ref[pl.ds(r0, L), c0:c0 + GM_GROUP_CH] = (gu[:, sl] * sp).astype(BF16)

        pre = gate_ref[pl.ds(r0, L), :] + gbias_ref[...]
        gl = jnp.where(lanes < HEADS, pre, jax.nn.log_sigmoid(pre))
        bc = _cumsum_rows(gl)
        gl_t = gl.T
        bc_t = bc.T

        qkbuf[QK_HALO:QK_HALO + L, :] = zc(Z_Q, 2 * MLSTM_CH)

        def short_conv(col):
            y = jnp.concatenate(
                [_causal_conv(qkbuf, qkw_ref, qkb_ref, slice(col + c * LANES, col + (c + 1) * LANES),
                              width=QK_WIDTH, halo=QK_HALO) for c in range(DH // LANES)],
                axis=-1)
            return y * _sigmoid(y)

        for h in range(HEADS):
            q_h = short_conv(h * DH)
            k_h = short_conv(MLSTM_CH + h * DH) * (DH ** -0.5)
            v_b = zc(Z_V + h * DH, DH).astype(BF16)
            o_h = zc(Z_O + h * DH, DH)
            li_col = gl[:, h:h + 1]
            b_col = bc[:, HEADS + h:HEADS + h + 1]
            li_row = gl_t[h:h + 1, :]
            b_row = bc_t[HEADS + h:HEADS + h + 1, :]
            m_prev = m_ref[h][:, 0:1]

            dmat = jnp.where(causal, b_col - b_row + li_row, -jnp.inf)
            inter = b_col + m_prev
            m_row = jnp.maximum(jnp.max(dmat, axis=-1, keepdims=True), inter)
            q_b = q_h.astype(BF16)
            s = lax.dot_general(q_b, k_h.astype(BF16), (((1,), (1,)), ((), ())),
                                preferred_element_type=F32)
            p = jnp.exp(dmat - m_row) * s
            gdec = jnp.exp(inter - m_row)
            c_old = c_ref[h]
            n_old = n_ref[h]
            num = (jnp.dot(p.astype(BF16), v_b, preferred_element_type=F32)
                   + gdec * jnp.dot(q_b, c_old.astype(BF16), preferred_element_type=F32))
            den = (jnp.sum(p, axis=-1, keepdims=True)
                   + gdec * jnp.sum(q_h * n_old, axis=-1, keepdims=True))
            hh = num / jnp.maximum(jnp.abs(den), jnp.exp(-m_row))

            b_last = b_col[L - 1:L, :]
            a_col = b_last - b_col + li_col
            m_new = jnp.maximum(b_last + m_prev, jnp.max(a_col, axis=0, keepdims=True))
            decay = jnp.exp(b_last + m_prev - m_new)
            kw = k_h * jnp.exp(a_col - m_new)
            c_ref[h] = decay * c_old + jnp.dot(kw.T.astype(BF16), v_b, preferred_element_type=F32)
            n_ref[h] = decay * n_old + jnp.sum(kw, axis=0, keepdims=True)
            m_ref[h] = jnp.broadcast_to(m_new, (1, LANES))

            hn = _rms(hh, mng_ref[:, h * DH:(h + 1) * DH])
            c0 = MIX_B + h * DH
            mix_ref[pl.ds(r0, L), c0:c0 + DH] = (hn * _sigmoid(o_h)).astype(BF16)

        qkbuf[0:QK_HALO, :] = qkbuf[L:L + QK_HALO, :]
        return carry

    for ci in range(chunks):
        chunk_body(ci, 0)
        rs = slice(ci * L, (ci + 1) * L)
        o_ref[rs, :] = x_ref[rs, :] + jnp.dot(mix_ref[rs, :], wout_ref[...],
                                              preferred_element_type=F32)


def _mixer(z, gates, x2, p, *, batch, seq, chunks=2):
    m, d = x2.shape
    t = chunks * CHUNK
    nt = seq // t
    row = lambda b, c: (b * nt + c, 0)
    const = lambda a: pl.BlockSpec(a.shape, (lambda b, c: (0,) * a.ndim),
                                   pipeline_mode=pl.Buffered(1))
    small = [p["conv_w"], p["conv_b"], p["conv_norm_g"], p["conv_norm_b"], p["qk_conv_w"],
             p["qk_conv_b"], p["gate_bias"], p["mlstm_norm_g"], p["gm_norm_g"], p["gm_norm_b"],
             p["gm_w"], p["gm_bt"]]
    return pl.pallas_call(
        functools.partial(_mixer_kernel, chunks=chunks),
        grid=(batch, nt),
        in_specs=[
            pl.BlockSpec((t, Z_COLS), row),
            pl.BlockSpec((t, LANES), row),
            pl.BlockSpec((t, d), row),
            *[const(a) for a in small],
            const(p["w_out"]),
        ],
        out_specs=pl.BlockSpec((t, d), row),
        out_shape=jax.ShapeDtypeStruct((m, d), F32),
        scratch_shapes=[
            pltpu.VMEM((CONV_HALO + CHUNK, CONV_CH), F32),
            pltpu.VMEM((QK_HALO + CHUNK, 2 * MLSTM_CH), F32),
            pltpu.VMEM((HEADS, DH, DH), F32),
            pltpu.VMEM((HEADS, 1, DH), F32),
            pltpu.VMEM((HEADS, 1, LANES), F32),
            pltpu.VMEM((t, d), BF16),
        ],
        compiler_params=pltpu.CompilerParams(
            dimension_semantics=("arbitrary", "arbitrary"),
            vmem_limit_bytes=VMEM_LIMIT),
        name="mixer",
    )(z, gates, x2, *small, p["w_out"])


def _mlp_kernel(x_ref, g_ref, wu_ref, wd_ref, fg_ref, o_ref, xn_ref, *, final_norm):
    k = pl.program_id(1)

    @pl.when(k == 0)
    def _():
        x = x_ref[...]
        xn_ref[...] = _rms(x, g_ref[...]).astype(BF16)
        o_ref[...] = x

    hm = jnp.dot(xn_ref[...], wu_ref[...], preferred_element_type=F32)
    hm = jnp.maximum(hm, 0.0)
    o_ref[...] += jnp.dot((hm * hm).astype(BF16), wd_ref[...], preferred_element_type=F32)

    if final_norm:
        @pl.when(k == pl.num_programs(1) - 1)
        def _():
            o_ref[...] = _rms(o_ref[...], fg_ref[...])


def _mlp(x2, g, wu, wd, fg, *, final_norm, tm=1024, tf=512):
    m, d = x2.shape
    f = wu.shape[1]
    return pl.pallas_call(
        functools.partial(_mlp_kernel, final_norm=final_norm),
        grid=(m // tm, f // tf),
        in_specs=[
            pl.BlockSpec((tm, d), lambda i, k: (i, 0)),
            pl.BlockSpec((1, d), lambda i, k: (0, 0)),
            pl.BlockSpec((d, tf), lambda i, k: (0, k)),
            pl.BlockSpec((tf, d), lambda i, k: (k, 0)),
            pl.BlockSpec((1, d), lambda i, k: (0, 0)),
        ],
        out_specs=pl.BlockSpec((tm, d), lambda i, k: (i, 0)),
        out_shape=jax.ShapeDtypeStruct((m, d), F32),
        scratch_shapes=[pltpu.VMEM((tm, d), BF16)],
        compiler_params=pltpu.CompilerParams(
            dimension_semantics=("arbitrary", "arbitrary"),
            vmem_limit_bytes=VMEM_LIMIT),
        name="mlp",
    )(x2, g, wu, wd, fg)


def _prep_w_in(w):
    cv, cg, q, k, v, o, ig, fg, gu, gv = jnp.split(
        w, [512, 1024, 2048, 3072, 4096, 5120, 5124, 5128, 5640], axis=-1)
    main = jnp.concatenate([cv, cg, q, k, v, o, gu, gv], axis=-1).astype(BF16)
    gates = jnp.concatenate([ig, fg], axis=-1)
    gates = jnp.pad(gates, ((0, 0), (0, LANES - 2 * HEADS))).astype(BF16)
    return main, gates


def kernel(x, ln_mix_g, w_in, conv_w, conv_b, conv_norm_g, conv_norm_b, qk_conv_w, qk_conv_b,
           igate_b, fgate_b, mlstm_norm_g, gm_norm_g, gm_norm_b, gm_w, gm_b, w_out,
           ln_mlp_g, w_up, w_down, final_g):
    batch, seq, d = x.shape
    depth = w_in.shape[0]
    x2 = x.reshape(batch * seq, d)
    row = lambda a: a.reshape(1, -1)
    for l in range(depth):
        w_main, w_gate = _prep_w_in(w_in[l])
        z, gates = _in_proj(x2, row(ln_mix_g[l]), w_main, w_gate)
        gate_bias = jnp.pad(jnp.concatenate([igate_b[l], fgate_b[l]]), (0, LANES - 2 * HEADS))
        params = dict(
            conv_w=conv_w[l], conv_b=row(conv_b[l]), conv_norm_g=row(conv_norm_g[l]),
            conv_norm_b=row(conv_norm_b[l]), qk_conv_w=qk_conv_w[l], qk_conv_b=row(qk_conv_b[l]),
            gate_bias=row(gate_bias), mlstm_norm_g=row(mlstm_norm_g[l]),
            gm_norm_g=row(gm_norm_g[l]), gm_norm_b=row(gm_norm_b[l]), gm_w=gm_w[l],
            gm_bt=gm_b[l].T, w_out=w_out[l].astype(BF16))
        x2 = _mixer(z, gates, x2, params, batch=batch, seq=seq)
        x2 = _mlp(x2, row(ln_mlp_g[l]), w_up[l].astype(BF16), w_down[l].astype(BF16),
                  row(final_g), final_norm=(l == depth - 1))
    return x2.reshape(batch, seq, d)
```

```python
import functools

import jax
import jax.numpy as jnp
from jax import lax
from jax.experimental import pallas as pl
from jax.experimental.pallas import tpu as pltpu

F32 = jnp.float32
BF16 = jnp.bfloat16

EPS = 1e-6
CONV_CH = 512
CONV_WIDTH = 31
HEADS = 4
DH = 256
MLSTM_CH = HEADS * DH
QK_WIDTH = 4
CHUNK = 128
GM_GROUPS = 4
GM_CH = 512
GM_GROUP_CH = GM_CH // GM_GROUPS
LANES = 128
SUBLANES = 8

Z_CV = 0
Z_CG = Z_CV + CONV_CH
Z_Q = Z_CG + CONV_CH
Z_K = Z_Q + MLSTM_CH
Z_V = Z_K + MLSTM_CH
Z_O = Z_V + MLSTM_CH
Z_GATE = Z_O + MLSTM_CH
Z_GU = Z_GATE + 2 * HEADS
Z_GV = Z_GU + GM_CH
Z_COLS = Z_GV + GM_CH
Z_COLS_PADDED = -(-Z_COLS // LANES) * LANES
Z_COL_BLOCK = 512
MIX_A = 0
MIX_B = CONV_CH
MIX_C = CONV_CH + MLSTM_CH

CONV_HALO = 32
QK_HALO = 8

MLP_VMEM_LIMIT = 56 * 1024 * 1024
MIXER_VMEM_LIMIT = 60 * 1024 * 1024


def _rms(x, g):
    return x * lax.rsqrt(jnp.mean(x * x, axis=-1, keepdims=True) + EPS) * g


def _layernorm(x, g, b):
    mu = jnp.mean(x, axis=-1, keepdims=True)
    xc = x - mu
    var = jnp.mean(xc * xc, axis=-1, keepdims=True)
    return xc * lax.rsqrt(var + EPS) * g + b


def _sigmoid(x):
    return 0.5 * jnp.tanh(0.5 * x) + 0.5


def _cumsum_rows(x):
    rows = lax.broadcasted_iota(jnp.int32, x.shape, 0)
    shift = 1
    while shift < x.shape[0]:
        x = x + jnp.where(rows >= shift, pltpu.roll(x, shift, axis=0), 0.0)
        shift *= 2
    return x


def _shift_up(x, r):
    return x if r == 0 else pltpu.roll(x, x.shape[0] - r, axis=0)


def _causal_conv(buf, w_ref, b_ref, cols, *, width, halo):
    L = CHUNK
    x = buf[:, cols]
    acc = jnp.broadcast_to(b_ref[:, cols], (L, LANES))
    first = halo - (width - 1)
    for r in range(SUBLANES):
        starts = [s for s in range(first, halo + 1) if s % SUBLANES == r]
        if not starts:
            continue
        xs = _shift_up(x, r)
        for s in starts:
            k = s - first
            a0 = s - r
            acc = acc + w_ref[k:k + 1, cols] * xs[a0:a0 + L]
    return acc


def _project_chunk(x, g_ref, win_ref, z_ref):
    xn = _rms(x, g_ref[...]).astype(BF16)
    for c0 in range(0, Z_GATE, Z_COL_BLOCK):
        z_ref[:, c0:c0 + Z_COL_BLOCK] = jnp.dot(xn, win_ref[:, c0:c0 + Z_COL_BLOCK],
                                                preferred_element_type=F32)
    z_ref[:, Z_GATE:Z_COLS] = jnp.dot(xn, win_ref[:, Z_GATE:Z_COLS], preferred_element_type=F32)


def _mix_chunk(z_ref, mix_ref, prm, state):
    (convw_ref, convb_ref, cng_ref, cnb_ref, qkw_ref, qkb_ref, gbias_ref, mng_ref,
     gng_ref, gnb_ref, gmw_ref, gmbt_ref) = prm
    abuf, qkbuf, c_ref, n_ref, m_ref = state
    L = CHUNK
    rows = lax.broadcasted_iota(jnp.int32, (L, L), 0)
    cols = lax.broadcasted_iota(jnp.int32, (L, L), 1)
    causal = cols <= rows
    lanes = lax.broadcasted_iota(jnp.int32, (L, LANES), 1)
    zc = lambda c0, n: z_ref[:, c0:c0 + n]

    abuf[CONV_HALO:CONV_HALO + L, :] = zc(Z_CV, CONV_CH) * _sigmoid(zc(Z_CG, CONV_CH))
    acc = jnp.concatenate(
        [_causal_conv(abuf, convw_ref, convb_ref, slice(c * LANES, (c + 1) * LANES),
                      width=CONV_WIDTH, halo=CONV_HALO) for c in range(CONV_CH // LANES)],
        axis=-1)
    abuf[0:CONV_HALO, :] = abuf[L:L + CONV_HALO, :]
    a = _layernorm(acc, cng_ref[...], cnb_ref[...])
    mix_ref[:, MIX_A:MIX_A + CONV_CH] = (a * _sigmoid(a)).astype(BF16)

    gu = jax.nn.gelu(zc(Z_GU, GM_CH))
    gv = _layernorm(jax.nn.gelu(zc(Z_GV, GM_CH)), gng_ref[...], gnb_ref[...])
    for g in range(GM_GROUPS):
        sl = slice(g * GM_GROUP_CH, (g + 1) * GM_GROUP_CH)
        wc = jnp.where(causal, gmw_ref[g], 0.0).astype(BF16)
        sp = jnp.dot(wc, gv[:, sl].astype(BF16), preferred_element_type=F32)
        sp = sp + gmbt_ref[:, g:g + 1]
        c0 = MIX_C + g * GM_GROUP_CH
        mix_ref[:, c0:c0 + GM_GROUP_CH] = (gu[:, sl] * sp).astype(BF16)

    pre = zc(Z_GATE, LANES) + gbias_ref[...]
    gl = jnp.where(lanes < HEADS, pre, jax.nn.log_sigmoid(pre))
    bc = _cumsum_rows(gl)
    gl_t = gl.T
    bc_t = bc.T

    qkbuf[QK_HALO:QK_HALO + L, :] = zc(Z_Q, 2 * MLSTM_CH)

    def short_conv(col):
        y = jnp.concatenate(
            [_causal_conv(qkbuf, qkw_ref, qkb_ref, slice(col + c * LANES, col + (c + 1) * LANES),
                          width=QK_WIDTH, halo=QK_HALO) for c in range(DH // LANES)],
            axis=-1)
        return y * _sigmoid(y)

    for h in range(HEADS):
        q_h = short_conv(h * DH)
        k_h = short_conv(MLSTM_CH + h * DH) * (DH ** -0.5)
        v_b = zc(Z_V + h * DH, DH).astype(BF16)
        o_h = zc(Z_O + h * DH, DH)
        li_col = gl[:, h:h + 1]
        b_col = bc[:, HEADS + h:HEADS + h + 1]
        li_row = gl_t[h:h + 1, :]
        b_row = bc_t[HEADS + h:HEADS + h + 1, :]
        m_prev = m_ref[h][:, 0:1]

        dmat = jnp.where(causal, b_col - b_row + li_row, -jnp.inf)
        inter = b_col + m_prev
        m_row = jnp.maximum(jnp.max(dmat, axis=-1, keepdims=True), inter)
        q_b = q_h.astype(BF16)
        s = lax.dot_general(q_b, k_h.astype(BF16), (((1,), (1,)), ((), ())),
                            preferred_element_type=F32)
        p = jnp.exp(dmat - m_row) * s
        gdec = jnp.exp(inter - m_row)
        c_old = c_ref[h]
        n_old = n_ref[h]
        num = (jnp.dot(p.astype(BF16), v_b, preferred_element_type=F32)
               + gdec * jnp.dot(q_b, c_old.astype(BF16), preferred_element_type=F32))
        den = (jnp.sum(p, axis=-1, keepdims=True)
               + gdec * jnp.sum(q_h * n_old, axis=-1, keepdims=True))
        hh = num / jnp.maximum(jnp.abs(den), jnp.exp(-m_row))

        b_last = b_col[L - 1:L, :]
        a_col = b_last - b_col + li_col
        m_new = jnp.maximum(b_last + m_prev, jnp.max(a_col, axis=0, keepdims=True))
        decay = jnp.exp(b_last + m_prev - m_new)
        kw = k_h * jnp.exp(a_col - m_new)
        c_ref[h] = decay * c_old + jnp.dot(kw.T.astype(BF16), v_b, preferred_element_type=F32)
        n_ref[h] = decay * n_old + jnp.sum(kw, axis=0, keepdims=True)
        m_ref[h] = jnp.broadcast_to(m_new, (1, LANES))

        hn = _rms(hh, mng_ref[:, h * DH:(h + 1) * DH])
        c0 = MIX_B + h * DH
        mix_ref[:, c0:c0 + DH] = (hn * _sigmoid(o_h)).astype(BF16)

    qkbuf[0:QK_HALO, :] = qkbuf[L:L + QK_HALO, :]


def _mixer_layer_kernel(x_ref, xnext_ref, g_ref, win_ref, *rest, steps_per_seq):
    prm = rest[:12]
    wout_ref, o_ref = rest[12:14]
    za, zb, mixa, mixb = rest[14:18]
    state = rest[18:]
    abuf, qkbuf, c_ref, n_ref, m_ref = state
    L = CHUNK
    s = pl.program_id(0)

    @pl.when(s == 0)
    def _():
        _project_chunk(x_ref[0:L, :], g_ref, win_ref, za)

    @pl.when(lax.rem(s, steps_per_seq) == 0)
    def _():
        abuf[0:CONV_HALO, :] = jnp.zeros((CONV_HALO, CONV_CH), F32)
        qkbuf[0:QK_HALO, :] = jnp.zeros((QK_HALO, 2 * MLSTM_CH), F32)
        c_ref[...] = jnp.zeros_like(c_ref)
        n_ref[...] = jnp.zeros_like(n_ref)
        m_ref[...] = jnp.zeros_like(m_ref)

    _project_chunk(x_ref[L:2 * L, :], g_ref, win_ref, zb)
    _mix_chunk(za, mixa, prm, state)
    o_ref[0:L, :] = x_ref[0:L, :] + jnp.dot(mixa[...], wout_ref[...], preferred_element_type=F32)
    _project_chunk(xnext_ref[...], g_ref, win_ref, za)
    _mix_chunk(zb, mixb, prm, state)
    o_ref[L:2 * L, :] = x_ref[L:2 * L, :] + jnp.dot(mixb[...], wout_ref[...],
                                                    preferred_element_type=F32)


def _mixer_layer(x2, layer, p, *, seq):
    m, d = x2.shape
    t = 2 * CHUNK
    steps = m // t
    last_chunk = m // CHUNK - 1

    def stacked(a):
        return pl.BlockSpec((None,) + a.shape[1:], (lambda s: (layer,) + (0,) * (a.ndim - 1)),
                            pipeline_mode=pl.Buffered(1))

    names = ["ln_mix_g", "w_in", "conv_w", "conv_b", "conv_norm_g", "conv_norm_b", "qk_conv_w",
             "qk_conv_b", "gate_bias", "mlstm_norm_g", "gm_norm_g", "gm_norm_b", "gm_w", "gm_bt",
             "w_out"]
    args = [p[k] for k in names]
    return pl.pallas_call(
        functools.partial(_mixer_layer_kernel, steps_per_seq=seq // t),
        grid=(steps,),
        in_specs=[
            pl.BlockSpec((t, d), lambda s: (s, 0)),
            pl.BlockSpec((CHUNK, d), lambda s: (jnp.minimum(2 * s + 2, last_chunk), 0)),
            *[stacked(a) for a in args],
        ],
        out_specs=pl.BlockSpec((t, d), lambda s: (s, 0)),
        out_shape=jax.ShapeDtypeStruct((m, d), F32),
        scratch_shapes=[
            pltpu.VMEM((CHUNK, Z_COLS_PADDED), F32),
            pltpu.VMEM((CHUNK, Z_COLS_PADDED), F32),
            pltpu.VMEM((CHUNK, d), BF16),
            pltpu.VMEM((CHUNK, d), BF16),
            pltpu.VMEM((CONV_HALO + CHUNK, CONV_CH), F32),
            pltpu.VMEM((QK_HALO + CHUNK, 2 * MLSTM_CH), F32),
            pltpu.VMEM((HEADS, DH, DH), F32),
            pltpu.VMEM((HEADS, 1, DH), F32),
            pltpu.VMEM((HEADS, 1, LANES), F32),
        ],
        compiler_params=pltpu.CompilerParams(
            dimension_semantics=("arbitrary",),
            vmem_limit_bytes=MIXER_VMEM_LIMIT),
        name="mixer_layer",
    )(x2, x2, *args)


def _mlp_kernel(x_ref, g_ref, wu_ref, wd_ref, fg_ref, o_ref, xn_ref, *, final_norm):
    k = pl.program_id(1)

    @pl.when(k == 0)
    def _():
        x = x_ref[...]
        xn_ref[...] = _rms(x, g_ref[...]).astype(BF16)
        o_ref[...] = x

    hm = jnp.dot(xn_ref[...], wu_ref[...], preferred_element_type=F32)
    hm = jnp.maximum(hm, 0.0)
    o_ref[...] += jnp.dot((hm * hm).astype(BF16), wd_ref[...], preferred_element_type=F32)

    if final_norm:
        @pl.when(k == pl.num_programs(1) - 1)
        def _():
            o_ref[...] = _rms(o_ref[...], fg_ref[...])


def _mlp(x2, layer, g, wu, wd, fg, *, final_norm, tm=1024, tf=512):
    m, d = x2.shape
    f = wu.shape[2]
    return pl.pallas_call(
        functools.partial(_mlp_kernel, final_norm=final_norm),
        grid=(m // tm, f // tf),
        in_specs=[
            pl.BlockSpec((tm, d), lambda i, k: (i, 0)),
            pl.BlockSpec((None, 1, d), lambda i, k: (layer, 0, 0)),
            pl.BlockSpec((None, d, tf), lambda i, k: (layer, 0, k)),
            pl.BlockSpec((None, tf, d), lambda i, k: (layer, k, 0)),
            pl.BlockSpec((1, d), lambda i, k: (0, 0)),
        ],
        out_specs=pl.BlockSpec((tm, d), lambda i, k: (i, 0)),
        out_shape=jax.ShapeDtypeStruct((m, d), F32),
        scratch_shapes=[pltpu.VMEM((tm, d), BF16)],
        compiler_params=pltpu.CompilerParams(
            dimension_semantics=("arbitrary", "arbitrary"),
            vmem_limit_bytes=MLP_VMEM_LIMIT),
        name="mlp",
    )(x2, g, wu, wd, fg)


def kernel(x, ln_mix_g, w_in, conv_w, conv_b, conv_norm_g, conv_norm_b, qk_conv_w, qk_conv_b,
           igate_b, fgate_b, mlstm_norm_g, gm_norm_g, gm_norm_b, gm_w, gm_b, w_out,
           ln_mlp_g, w_up, w_down, final_g):
    batch, seq, d = x.shape
    depth = w_in.shape[0]
    x2 = x.reshape(batch * seq, d)
    rows = lambda a: a.reshape(depth, 1, -1)
    gate_bias = jnp.pad(jnp.concatenate([igate_b, fgate_b], axis=-1),
                        ((0, 0), (0, LANES - 2 * HEADS)))
    params = dict(
        ln_mix_g=rows(ln_mix_g), w_in=w_in.astype(BF16), conv_w=conv_w, conv_b=rows(conv_b),
        conv_norm_g=rows(conv_norm_g), conv_norm_b=rows(conv_norm_b), qk_conv_w=qk_conv_w,
        qk_conv_b=rows(qk_conv_b), gate_bias=rows(gate_bias), mlstm_norm_g=rows(mlstm_norm_g),
        gm_norm_g=rows(gm_norm_g), gm_norm_b=rows(gm_norm_b), gm_w=gm_w,
        gm_bt=jnp.swapaxes(gm_b, 1, 2), w_out=w_out.astype(BF16))
    w_up_b = w_up.astype(BF16)
    w_down_b = w_down.astype(BF16)
    for l in range(depth):
        x2 = _mixer_layer(x2, l, params, seq=seq)
        x2 = _mlp(x2, l, rows(ln_mlp_g), w_up_b, w_down_b, final_g.reshape(1, -1),
                  final_norm=(l == depth - 1))
    return x2.reshape(batch, seq, d)
```

```python
import functools

import jax
import jax.numpy as jnp
from jax import lax
from jax.experimental import pallas as pl
from jax.experimental.pallas import tpu as pltpu

F32 = jnp.float32
BF16 = jnp.bfloat16

EPS = 1e-6
CONV_CH = 512
CONV_WIDTH = 31
HEADS = 4
DH = 256
MLSTM_CH = HEADS * DH
QK_WIDTH = 4
CHUNK = 128
GM_GROUPS = 4
GM_CH = 512
GM_GROUP_CH = GM_CH // GM_GROUPS
LANES = 128
SUBLANES = 8

Z_CV = 0
Z_CG = Z_CV + CONV_CH
Z_Q = Z_CG + CONV_CH
Z_K = Z_Q + MLSTM_CH
Z_V = Z_K + MLSTM_CH
Z_O = Z_V + MLSTM_CH
Z_GATE = Z_O + MLSTM_CH
Z_GU = Z_GATE + 2 * HEADS
Z_GV = Z_GU + GM_CH
Z_COLS = Z_GV + GM_CH
Z_COLS_PADDED = -(-Z_COLS // LANES) * LANES
Z_COL_BLOCK = 512
MIX_A = 0
MIX_B = CONV_CH
MIX_C = CONV_CH + MLSTM_CH

CONV_HALO = 32
QK_HALO = 8

MLP_VMEM_LIMIT = 56 * 1024 * 1024
MIXER_VMEM_LIMIT = 60 * 1024 * 1024


def _rms(x, g):
    return x * lax.rsqrt(jnp.mean(x * x, axis=-1, keepdims=True) + EPS) * g


def _layernorm(x, g, b):
    mu = jnp.mean(x, axis=-1, keepdims=True)
    xc = x - mu
    var = jnp.mean(xc * xc, axis=-1, keepdims=True)
    return xc * lax.rsqrt(var + EPS) * g + b


def _sigmoid(x):
    return 0.5 * jnp.tanh(0.5 * x) + 0.5


def _cumsum_rows(x):
    rows = lax.broadcasted_iota(jnp.int32, x.shape, 0)
    shift = 1
    while shift < x.shape[0]:
        x = x + jnp.where(rows >= shift, pltpu.roll(x, shift, axis=0), 0.0)
        shift *= 2
    return x


def _shift_up(x, r):
    return x if r == 0 else pltpu.roll(x, x.shape[0] - r, axis=0)


def _causal_conv(buf, w_ref, b_ref, cols, *, width, halo):
    L = CHUNK
    x = buf[:, cols]
    acc = jnp.broadcast_to(b_ref[:, cols], (L, LANES))
    first = halo - (width - 1)
    for r in range(SUBLANES):
        starts = [s for s in range(first, halo + 1) if s % SUBLANES == r]
        if not starts:
            continue
        xs = _shift_up(x, r)
        for s in starts:
            k = s - first
            a0 = s - r
            acc = acc + w_ref[k:k + 1, cols] * xs[a0:a0 + L]
    return acc


def _project_tasks(x_rows, g_ref, win_ref, xn_ref, z_ref):
    def norm():
        xn_ref[...] = _rms(x_rows(), g_ref[...]).astype(BF16)

    def block(c0, c1):
        def run():
            z_ref[:, c0:c1] = jnp.dot(xn_ref[...], win_ref[:, c0:c1], preferred_element_type=F32)
        return run

    starts = list(range(0, Z_GATE, Z_COL_BLOCK))
    return [norm] + [block(c0, c0 + Z_COL_BLOCK) for c0 in starts] + [block(Z_GATE, Z_COLS)]


def _out_proj_tasks(x_rows, mix_ref, wout_ref, o_rows):
    def block(c0):
        def run():
            o_rows(c0)[...] = x_rows(c0)[...] + jnp.dot(
                mix_ref[...], wout_ref[:, c0:c0 + Z_COL_BLOCK], preferred_element_type=F32)
        return run
    return [block(c0) for c0 in range(0, wout_ref.shape[1], Z_COL_BLOCK)]


def _interleave(stages, tasks):
    total = sum(MIX_STAGE_COST)
    done = 0.0
    issued = 0
    for cost in stages:
        done += cost
        target = round(len(tasks) * done / total)
        while issued < target:
            tasks[issued]()
            issued += 1
    for t in tasks[issued:]:
        t()


MIX_STAGE_COST = (17, 9, 3) + (9,) * HEADS


def _mix_stages(z_ref, mix_ref, prm, state):
    costs = iter(MIX_STAGE_COST)
    (convw_ref, convb_ref, cng_ref, cnb_ref, qkw_ref, qkb_ref, gbias_ref, mng_ref,
     gng_ref, gnb_ref, gmw_ref, gmbt_ref) = prm
    abuf, qkbuf, c_ref, n_ref, m_ref = state
    L = CHUNK
    rows = lax.broadcasted_iota(jnp.int32, (L, L), 0)
    cols = lax.broadcasted_iota(jnp.int32, (L, L), 1)
    causal = cols <= rows
    lanes = lax.broadcasted_iota(jnp.int32, (L, LANES), 1)
    zc = lambda c0, n: z_ref[:, c0:c0 + n]

    abuf[CONV_HALO:CONV_HALO + L, :] = zc(Z_CV, CONV_CH) * _sigmoid(zc(Z_CG, CONV_CH))
    acc = jnp.concatenate(
        [_causal_conv(abuf, convw_ref, convb_ref, slice(c * LANES, (c + 1) * LANES),
                      width=CONV_WIDTH, halo=CONV_HALO) for c in range(CONV_CH // LANES)],
        axis=-1)
    abuf[0:CONV_HALO, :] = abuf[L:L + CONV_HALO, :]
    a = _layernorm(acc, cng_ref[...], cnb_ref[...])
    mix_ref[:, MIX_A:MIX_A + CONV_CH] = (a * _sigmoid(a)).astype(BF16)
    yield next(costs)

    gu = jax.nn.gelu(zc(Z_GU, GM_CH))
    gv = _layernorm(jax.nn.gelu(zc(Z_GV, GM_CH)), gng_ref[...], gnb_ref[...])
    for g in range(GM_GROUPS):
        sl = slice(g * GM_GROUP_CH, (g + 1) * GM_GROUP_CH)
        wc = jnp.where(causal, gmw_ref[g], 0.0).astype(BF16)
        sp = jnp.dot(wc, gv[:, sl].astype(BF16), preferred_element_type=F32)
        sp = sp + gmbt_ref[:, g:g + 1]
        c0 = MIX_C + g * GM_GROUP_CH
        mix_ref[:, c0:c0 + GM_GROUP_CH] = (gu[:, sl] * sp).astype(BF16)
    yield next(costs)

    pre = zc(Z_GATE, LANES) + gbias_ref[...]
    gl = jnp.where(lanes < HEADS, pre, jax.nn.log_sigmoid(pre))
    bc = _cumsum_rows(gl)
    gl_t = gl.T
    bc_t = bc.T

    qkbuf[QK_HALO:QK_HALO + L, :] = zc(Z_Q, 2 * MLSTM_CH)
    yield next(costs)

    def short_conv(col):
        y = jnp.concatenate(
            [_causal_conv(qkbuf, qkw_ref, qkb_ref, slice(col + c * LANES, col + (c + 1) * LANES),
                          width=QK_WIDTH, halo=QK_HALO) for c in range(DH // LANES)],
            axis=-1)
        return y * _sigmoid(y)

    for h in range(HEADS):
        q_h = short_conv(h * DH)
        k_h = short_conv(MLSTM_CH + h * DH) * (DH ** -0.5)
        v_b = zc(Z_V + h * DH, DH).astype(BF16)
        o_h = zc(Z_O + h * DH, DH)
        li_col = gl[:, h:h + 1]
        b_col = bc[:, HEADS + h:HEADS + h + 1]
        li_row = gl_t[h:h + 1, :]
        b_row = bc_t[HEADS + h:HEADS + h + 1, :]
        m_prev = m_ref[h][:, 0:1]

        dmat = jnp.where(causal, b_col - b_row + li_row, -jnp.inf)
        inter = b_col + m_prev
        m_row = jnp.maximum(jnp.max(dmat, axis=-1, keepdims=True), inter)
        q_b = q_h.astype(BF16)
        s = lax.dot_general(q_b, k_h.astype(BF16), (((1,), (1,)), ((), ())),
                            preferred_element_type=F32)
        p = jnp.exp(dmat - m_row) * s
        gdec = jnp.exp(inter - m_row)
        c_old = c_ref[h]
        n_old = n_ref[h]
        num = (jnp.dot(p.astype(BF16), v_b, preferred_element_type=F32)
               + gdec * jnp.dot(q_b, c_old.astype(BF16), preferred_element_type=F32))
        den = (jnp.sum(p, axis=-1, keepdims=True)
               + gdec * jnp.sum(q_h * n_old, axis=-1, keepdims=True))
        hh = num / jnp.maximum(jnp.abs(den), jnp.exp(-m_row))

        b_last = b_col[L - 1:L, :]
        a_col = b_last - b_col + li_col
        m_new = jnp.maximum(b_last + m_prev, jnp.max(a_col, axis=0, keepdims=True))
        decay = jnp.exp(b_last + m_prev - m_new)
        kw = k_h * jnp.exp(a_col - m_new)
        c_ref[h] = decay * c_old + jnp.dot(kw.T.astype(BF16), v_b, preferred_element_type=F32)
        n_ref[h] = decay * n_old + jnp.sum(kw, axis=0, keepdims=True)
        m_ref[h] = jnp.broadcast_to(m_new, (1, LANES))

        hn = _rms(hh, mng_ref[:, h * DH:(h + 1) * DH])
        c0 = MIX_B + h * DH
        mix_ref[:, c0:c0 + DH] = (hn * _sigmoid(o_h)).astype(BF16)
        if h == HEADS - 1:
            qkbuf[0:QK_HALO, :] = qkbuf[L:L + QK_HALO, :]
        yield next(costs)


def _mixer_layer_kernel(x_ref, xnext_ref, g_ref, win_ref, *rest, steps_per_seq):
    prm = rest[:12]
    wout_ref, o_ref = rest[12:14]
    za, zb, xna, xnb, mixa, mixb = rest[14:20]
    state = rest[20:]
    abuf, qkbuf, c_ref, n_ref, m_ref = state
    L = CHUNK
    s = pl.program_id(0)
    rows0 = lambda c0: (slice(0, L), slice(c0, c0 + Z_COL_BLOCK))
    rows1 = lambda c0: (slice(L, 2 * L), slice(c0, c0 + Z_COL_BLOCK))

    @pl.when(s == 0)
    def _():
        for task in _project_tasks(lambda: x_ref[0:L, :], g_ref, win_ref, xna, za):
            task()

    @pl.when(lax.rem(s, steps_per_seq) == 0)
    def _():
        abuf[0:CONV_HALO, :] = jnp.zeros((CONV_HALO, CONV_CH), F32)
        qkbuf[0:QK_HALO, :] = jnp.zeros((QK_HALO, 2 * MLSTM_CH), F32)
        c_ref[...] = jnp.zeros_like(c_ref)
        n_ref[...] = jnp.zeros_like(n_ref)
        m_ref[...] = jnp.zeros_like(m_ref)

    _interleave(_mix_stages(za, mixa, prm, state),
                _project_tasks(lambda: x_ref[L:2 * L, :], g_ref, win_ref, xnb, zb))
    _interleave(_mix_stages(zb, mixb, prm, state),
                _out_proj_tasks(lambda c0: x_ref.at[rows0(c0)], mixa, wout_ref,
                                lambda c0: o_ref.at[rows0(c0)])
                + _project_tasks(lambda: xnext_ref[...], g_ref, win_ref, xna, za))
    for task in _out_proj_tasks(lambda c0: x_ref.at[rows1(c0)], mixb, wout_ref,
                                lambda c0: o_ref.at[rows1(c0)]):
        task()


def _mixer_layer(x2, layer, p, *, seq):
    m, d = x2.shape
    t = 2 * CHUNK
    steps = m // t
    last_chunk = m // CHUNK - 1

    def stacked(a):
        return pl.BlockSpec((None,) + a.shape[1:], (lambda s: (layer,) + (0,) * (a.ndim - 1)),
                            pipeline_mode=pl.Buffered(1))

    names = ["ln_mix_g", "w_in", "conv_w", "conv_b", "conv_norm_g", "conv_norm_b", "qk_conv_w",
             "qk_conv_b", "gate_bias", "mlstm_norm_g", "gm_norm_g", "gm_norm_b", "gm_w", "gm_bt",
             "w_out"]
    args = [p[k] for k in names]
    return pl.pallas_call(
        functools.partial(_mixer_layer_kernel, steps_per_seq=seq // t),
        grid=(steps,),
        in_specs=[
            pl.BlockSpec((t, d), lambda s: (s, 0)),
            pl.BlockSpec((CHUNK, d), lambda s: (jnp.minimum(2 * s + 2, last_chunk), 0)),
            *[stacked(a) for a in args],
        ],
        out_specs=pl.BlockSpec((t, d), lambda s: (s, 0)),
        out_shape=jax.ShapeDtypeStruct((m, d), F32),
        scratch_shapes=[
            pltpu.VMEM((CHUNK, Z_COLS_PADDED), F32),
            pltpu.VMEM((CHUNK, Z_COLS_PADDED), F32),
            pltpu.VMEM((CHUNK, d), BF16),
            pltpu.VMEM((CHUNK, d), BF16),
            pltpu.VMEM((CHUNK, d), BF16),
            pltpu.VMEM((CHUNK, d), BF16),
            pltpu.VMEM((CONV_HALO + CHUNK, CONV_CH), F32),
            pltpu.VMEM((QK_HALO + CHUNK, 2 * MLSTM_CH), F32),
            pltpu.VMEM((HEADS, DH, DH), F32),
            pltpu.VMEM((HEADS, 1, DH), F32),
            pltpu.VMEM((HEADS, 1, LANES), F32),
        ],
        compiler_params=pltpu.CompilerParams(
            dimension_semantics=("arbitrary",),
            vmem_limit_bytes=MIXER_VMEM_LIMIT),
        name="mixer_layer",
    )(x2, x2, *args)


def _mlp_kernel(x_ref, g_ref, wu_ref, wd_ref, fg_ref, o_ref, xn_ref, *, final_norm):
    k = pl.program_id(1)

    @pl.when(k == 0)
    def _():
        x = x_ref[...]
        xn_ref[...] = _rms(x, g_ref[...]).astype(BF16)
        o_ref[...] = x

    hm = jnp.dot(xn_ref[...], wu_ref[...], preferred_element_type=F32)
    hm = jnp.maximum(hm, 0.0)
    o_ref[...] += jnp.dot((hm * hm).astype(BF16), wd_ref[...], preferred_element_type=F32)

    if final_norm:
        @pl.when(k == pl.num_programs(1) - 1)
        def _():
            o_ref[...] = _rms(o_ref[...], fg_ref[...])


def _mlp(x2, layer, g, wu, wd, fg, *, final_norm, tm=1024, tf=512):
    m, d = x2.shape
    f = wu.shape[2]
    return pl.pallas_call(
        functools.partial(_mlp_kernel, final_norm=final_norm),
        grid=(m // tm, f // tf),
        in_specs=[
            pl.BlockSpec((tm, d), lambda i, k: (i, 0)),
            pl.BlockSpec((None, 1, d), lambda i, k: (layer, 0, 0)),
            pl.BlockSpec((None, d, tf), lambda i, k: (layer, 0, k)),
            pl.BlockSpec((None, tf, d), lambda i, k: (layer, k, 0)),
            pl.BlockSpec((1, d), lambda i, k: (0, 0)),
        ],
        out_specs=pl.BlockSpec((tm, d), lambda i, k: (i, 0)),
        out_shape=jax.ShapeDtypeStruct((m, d), F32),
        scratch_shapes=[pltpu.VMEM((tm, d), BF16)],
        compiler_params=pltpu.CompilerParams(
            dimension_semantics=("arbitrary", "arbitrary"),
            vmem_limit_bytes=MLP_VMEM_LIMIT),
        name="mlp",
    )(x2, g, wu, wd, fg)


def kernel(x, ln_mix_g, w_in, conv_w, conv_b, conv_norm_g, conv_norm_b, qk_conv_w, qk_conv_b,
           igate_b, fgate_b, mlstm_norm_g, gm_norm_g, gm_norm_b, gm_w, gm_b, w_out,
           ln_mlp_g, w_up, w_down, final_g):
    batch, seq, d = x.shape
    depth = w_in.shape[0]
    x2 = x.reshape(batch * seq, d)
    rows = lambda a: a.reshape(depth, 1, -1)
    gate_bias = jnp.pad(jnp.concatenate([igate_b, fgate_b], axis=-1),
                        ((0, 0), (0, LANES - 2 * HEADS)))
    w_in_b = jnp.pad(w_in, ((0, 0), (0, 0), (0, Z_COLS_PADDED - Z_COLS))).astype(BF16)
    params = dict(
        ln_mix_g=rows(ln_mix_g), w_in=w_in_b, conv_w=conv_w, conv_b=rows(conv_b),
        conv_norm_g=rows(conv_norm_g), conv_norm_b=rows(conv_norm_b), qk_conv_w=qk_conv_w,
        qk_conv_b=rows(qk_conv_b), gate_bias=rows(gate_bias), mlstm_norm_g=rows(mlstm_norm_g),
        gm_norm_g=rows(gm_norm_g), gm_norm_b=rows(gm_norm_b), gm_w=gm_w,
        gm_bt=jnp.swapaxes(gm_b, 1, 2), w_out=w_out.astype(BF16))
    w_up_b = w_up.astype(BF16)
    w_down_b = w_down.astype(BF16)
    for l in range(depth):
        x2 = _mixer_layer(x2, l, params, seq=seq)
        x2 = _mlp(x2, l, rows(ln_mlp_g), w_up_b, w_down_b, final_g.reshape(1, -1),
                  final_norm=(l == depth - 1))
    return x2.reshape(batch, seq, d)
```

```python
import functools

import jax
import jax.numpy as jnp
from jax import lax
from jax.experimental import pallas as pl
from jax.experimental.pallas import tpu as pltpu

F32 = jnp.float32
BF16 = jnp.bfloat16

EPS = 1e-6
CONV_CH = 512
CONV_WIDTH = 31
HEADS = 4
DH = 256
MLSTM_CH = HEADS * DH
QK_WIDTH = 4
CHUNK = 128
GM_GROUPS = 4
GM_CH = 512
GM_GROUP_CH = GM_CH // GM_GROUPS
LANES = 128
SUBLANES = 8

Z_CV = 0
Z_CG = Z_CV + CONV_CH
Z_Q = Z_CG + CONV_CH
Z_K = Z_Q + MLSTM_CH
Z_V = Z_K + MLSTM_CH
Z_O = Z_V + MLSTM_CH
Z_GATE = Z_O + MLSTM_CH
Z_GU = Z_GATE + 2 * HEADS
Z_GV = Z_GU + GM_CH
Z_COLS = Z_GV + GM_CH
Z_COLS_PADDED = -(-Z_COLS // LANES) * LANES
Z_COL_BLOCK = 512
MIX_A = 0
MIX_B = CONV_CH
MIX_C = CONV_CH + MLSTM_CH

CONV_HALO = 32
QK_HALO = 8

MLP_VMEM_LIMIT = 56 * 1024 * 1024
MIXER_VMEM_LIMIT = 60 * 1024 * 1024


def _rms(x, g):
    return x * lax.rsqrt(jnp.mean(x * x, axis=-1, keepdims=True) + EPS) * g


def _layernorm(x, g, b):
    mu = jnp.mean(x, axis=-1, keepdims=True)
    xc = x - mu
    var = jnp.mean(xc * xc, axis=-1, keepdims=True)
    return xc * lax.rsqrt(var + EPS) * g + b


def _gate(v, x):
    hv = 0.5 * v
    return hv * jnp.tanh(0.5 * x) + hv


def _silu(x):
    h = 0.5 * x
    return h * jnp.tanh(h) + h


def _cumsum_rows(x):
    rows = lax.broadcasted_iota(jnp.int32, x.shape, 0)
    shift = 1
    while shift < x.shape[0]:
        x = x + jnp.where(rows >= shift, pltpu.roll(x, shift, axis=0), 0.0)
        shift *= 2
    return x


def _shift_up(x, r):
    return x if r == 0 else pltpu.roll(x, x.shape[0] - r, axis=0)


def _causal_conv(buf, w_ref, b_ref, cols, *, width, halo):
    L = CHUNK
    x = buf[:, cols]
    acc = jnp.broadcast_to(b_ref[:, cols], (L, LANES))
    first = halo - (width - 1)
    for r in range(SUBLANES):
        starts = [s for s in range(first, halo + 1) if s % SUBLANES == r]
        if not starts:
            continue
        xs = _shift_up(x, r)
        for s in starts:
            k = s - first
            a0 = s - r
            acc = acc + w_ref[k:k + 1, cols] * xs[a0:a0 + L]
    return acc


def _short_conv_block(buf, w_ref, b_ref, cols):
    L = CHUNK
    n = QK_HALO + L
    x = buf[:, cols]
    x1 = pltpu.roll(x, 1, axis=0)
    w = lambda k: w_ref[k:k + 1, cols]
    u = w(3) * x + w(2) * x1
    v = w(1) * x + w(0) * x1
    y = u + pltpu.roll(v, 2, axis=0)
    assert QK_WIDTH == 4 and QK_HALO >= 3 and n % SUBLANES == 0
    return b_ref[:, cols] + y[QK_HALO:n]


def _project_tasks(x_rows, g_ref, win_ref, xn_ref, z_ref):
    def norm():
        xn_ref[...] = _rms(x_rows(), g_ref[...]).astype(BF16)

    def block(c0, c1):
        def run():
            z_ref[:, c0:c1] = jnp.dot(xn_ref[...], win_ref[:, c0:c1], preferred_element_type=F32)
        return run

    starts = list(range(0, Z_GATE, Z_COL_BLOCK))
    return [norm] + [block(c0, c0 + Z_COL_BLOCK) for c0 in starts] + [block(Z_GATE, Z_COLS)]


def _out_proj_tasks(x_rows, mix_ref, wout_ref, o_rows):
    def block(c0):
        def run():
            o_rows(c0)[...] = x_rows(c0)[...] + jnp.dot(
                mix_ref[...], wout_ref[:, c0:c0 + Z_COL_BLOCK], preferred_element_type=F32)
        return run
    return [block(c0) for c0 in range(0, wout_ref.shape[1], Z_COL_BLOCK)]


def _interleave(stages, tasks):
    total = sum(MIX_STAGE_UNITS)
    done = 0.0
    issued = 0
    for cost in stages:
        done += cost
        target = round(len(tasks) * done / total)
        while issued < target:
            tasks[issued]()
            issued += 1
    for t in tasks[issued:]:
        t()


MIX_STAGE_UNITS = (5.0, 4.5, 1.5, 2.0, 1.5)


def _mix_stages(z_ref, mix_ref, prm, state):
    units = iter(MIX_STAGE_UNITS)
    yield next(units)
    (convw_ref, convb_ref, cng_ref, cnb_ref, qkw_ref, qkb_ref, gbias_ref, mng_ref,
     gng_ref, gnb_ref, gmw_ref, gmbt_ref) = prm
    abuf, qkbuf, c_ref, n_ref, m_ref = state
    L = CHUNK
    rows = lax.broadcasted_iota(jnp.int32, (L, L), 0)
    cols = lax.broadcasted_iota(jnp.int32, (L, L), 1)
    causal = cols <= rows
    lanes = lax.broadcasted_iota(jnp.int32, (L, LANES), 1)
    zc = lambda c0, n: z_ref[:, c0:c0 + n]

    abuf[CONV_HALO:CONV_HALO + L, :] = _gate(zc(Z_CV, CONV_CH), zc(Z_CG, CONV_CH))
    acc = jnp.concatenate(
        [_causal_conv(abuf, convw_ref, convb_ref, slice(c * LANES, (c + 1) * LANES),
                      width=CONV_WIDTH, halo=CONV_HALO) for c in range(CONV_CH // LANES)],
        axis=-1)
    abuf[0:CONV_HALO, :] = abuf[L:L + CONV_HALO, :]
    a = _layernorm(acc, cng_ref[...], cnb_ref[...])
    mix_ref[:, MIX_A:MIX_A + CONV_CH] = _silu(a).astype(BF16)

    gu = jax.nn.gelu(zc(Z_GU, GM_CH))
    gv = _layernorm(jax.nn.gelu(zc(Z_GV, GM_CH)), gng_ref[...], gnb_ref[...])
    for g in range(GM_GROUPS):
        sl = slice(g * GM_GROUP_CH, (g + 1) * GM_GROUP_CH)
        wc = jnp.where(causal, gmw_ref[g], 0.0).astype(BF16)
        sp = jnp.dot(wc, gv[:, sl].astype(BF16), preferred_element_type=F32)
        sp = sp + gmbt_ref[:, g:g + 1]
        c0 = MIX_C + g * GM_GROUP_CH
        mix_ref[:, c0:c0 + GM_GROUP_CH] = (gu[:, sl] * sp).astype(BF16)
    yield next(units)

    pre = zc(Z_GATE, LANES) + gbias_ref[...]
    gl = jnp.where(lanes < HEADS, pre, jax.nn.log_sigmoid(pre))
    bc = _cumsum_rows(gl)
    gl_t = gl.T
    bc_t = bc.T

    qkbuf[QK_HALO:QK_HALO + L, :] = zc(Z_Q, 2 * MLSTM_CH)

    def short_conv(col):
        y = jnp.concatenate(
            [_short_conv_block(qkbuf, qkw_ref, qkb_ref, slice(col + c * LANES, col + (c + 1) * LANES))
             for c in range(DH // LANES)], axis=-1)
        return _silu(y)

    hs = range(HEADS)
    q = [short_conv(h * DH) for h in hs]
    k = [short_conv(MLSTM_CH + h * DH) * (DH ** -0.5) for h in hs]
    qkbuf[0:QK_HALO, :] = qkbuf[L:L + QK_HALO, :]
    q_b = [q[h].astype(BF16) for h in hs]
    v_b = [zc(Z_V + h * DH, DH).astype(BF16) for h in hs]
    li_col = [gl[:, h:h + 1] for h in hs]
    b_col = [bc[:, HEADS + h:HEADS + h + 1] for h in hs]
    m_prev = [m_ref[h][:, 0:1] for h in hs]
    dmat, inter, m_row = [], [], []
    for h in hs:
        li_row = gl_t[h:h + 1, :]
        b_row = bc_t[HEADS + h:HEADS + h + 1, :]
        dmat.append(jnp.where(causal, b_col[h] - b_row + li_row, -jnp.inf))
        inter.append(b_col[h] + m_prev[h])
        m_row.append(jnp.maximum(jnp.max(dmat[h], axis=-1, keepdims=True), inter[h]))
    s = [lax.dot_general(q_b[h], k[h].astype(BF16), (((1,), (1,)), ((), ())),
                         preferred_element_type=F32) for h in hs]
    yield next(units)

    p = [jnp.exp(dmat[h] - m_row[h]) * s[h] for h in hs]
    gdec = [jnp.exp(inter[h] - m_row[h]) for h in hs]
    c_old = [c_ref[h] for h in hs]
    n_old = [n_ref[h] for h in hs]
    num = [jnp.dot(p[h].astype(BF16), v_b[h], preferred_element_type=F32)
           + gdec[h] * jnp.dot(q_b[h], c_old[h].astype(BF16), preferred_element_type=F32)
           for h in hs]
    yield next(units)

    hh = []
    for h in hs:
        den = (jnp.sum(p[h], axis=-1, keepdims=True)
               + gdec[h] * jnp.sum(q[h] * n_old[h], axis=-1, keepdims=True))
        hh.append(num[h] / jnp.maximum(jnp.abs(den), jnp.exp(-m_row[h])))
    for h in hs:
        b_last = b_col[h][L - 1:L, :]
        a_col = b_last - b_col[h] + li_col[h]
        m_new = jnp.maximum(b_last + m_prev[h], jnp.max(a_col, axis=0, keepdims=True))
        decay = jnp.exp(b_last + m_prev[h] - m_new)
        kw = k[h] * jnp.exp(a_col - m_new)
        c_ref[h] = decay * c_old[h] + jnp.dot(kw.T.astype(BF16), v_b[h],
                                              preferred_element_type=F32)
        n_ref[h] = decay * n_old[h] + jnp.sum(kw, axis=0, keepdims=True)
        m_ref[h] = jnp.broadcast_to(m_new, (1, LANES))
    yield next(units)

    for h in hs:
        hn = _rms(hh[h], mng_ref[:, h * DH:(h + 1) * DH])
        c0 = MIX_B + h * DH
        mix_ref[:, c0:c0 + DH] = _gate(hn, zc(Z_O + h * DH, DH)).astype(BF16)


def _mixer_layer_kernel(x_ref, xnext_ref, g_ref, win_ref, *rest, steps_per_seq):
    prm = rest[:12]
    wout_ref, o_ref = rest[12:14]
    za, zb, xna, xnb, mixa, mixb = rest[14:20]
    state = rest[20:]
    abuf, qkbuf, c_ref, n_ref, m_ref = state
    L = CHUNK
    s = pl.program_id(0)
    rows0 = lambda c0: (slice(0, L), slice(c0, c0 + Z_COL_BLOCK))
    rows1 = lambda c0: (slice(L, 2 * L), slice(c0, c0 + Z_COL_BLOCK))

    @pl.when(s == 0)
    def _():
        for task in _project_tasks(lambda: x_ref[0:L, :], g_ref, win_ref, xna, za):
            task()

    @pl.when(lax.rem(s, steps_per_seq) == 0)
    def _():
        abuf[0:CONV_HALO, :] = jnp.zeros((CONV_HALO, CONV_CH), F32)
        qkbuf[0:QK_HALO, :] = jnp.zeros((QK_HALO, 2 * MLSTM_CH), F32)
        c_ref[...] = jnp.zeros_like(c_ref)
        n_ref[...] = jnp.zeros_like(n_ref)
        m_ref[...] = jnp.zeros_like(m_ref)

    _interleave(_mix_stages(za, mixa, prm, state),
                _project_tasks(lambda: x_ref[L:2 * L, :], g_ref, win_ref, xnb, zb))
    _interleave(_mix_stages(zb, mixb, prm, state),
                _out_proj_tasks(lambda c0: x_ref.at[rows0(c0)], mixa, wout_ref,
                                lambda c0: o_ref.at[rows0(c0)])
                + _project_tasks(lambda: xnext_ref[...], g_ref, win_ref, xna, za))
    for task in _out_proj_tasks(lambda c0: x_ref.at[rows1(c0)], mixb, wout_ref,
                                lambda c0: o_ref.at[rows1(c0)]):
        task()


def _mixer_layer(x2, layer, p, *, seq):
    m, d = x2.shape
    t = 2 * CHUNK
    steps = m // t
    last_chunk = m // CHUNK - 1

    def stacked(a):
        return pl.BlockSpec((None,) + a.shape[1:], (lambda s: (layer,) + (0,) * (a.ndim - 1)),
                            pipeline_mode=pl.Buffered(1))

    names = ["ln_mix_g", "w_in", "conv_w", "conv_b", "conv_norm_g", "conv_norm_b", "qk_conv_w",
             "qk_conv_b", "gate_bias", "mlstm_norm_g", "gm_norm_g", "gm_norm_b", "gm_w", "gm_bt",
             "w_out"]
    args = [p[k] for k in names]
    return pl.pallas_call(
        functools.partial(_mixer_layer_kernel, steps_per_seq=seq // t),
        grid=(steps,),
        in_specs=[
            pl.BlockSpec((t, d), lambda s: (s, 0)),
            pl.BlockSpec((CHUNK, d), lambda s: (jnp.minimum(2 * s + 2, last_chunk), 0)),
            *[stacked(a) for a in args],
        ],
        out_specs=pl.BlockSpec((t, d), lambda s: (s, 0)),
        out_shape=jax.ShapeDtypeStruct((m, d), F32),
        scratch_shapes=[
            pltpu.VMEM((CHUNK, Z_COLS_PADDED), F32),
            pltpu.VMEM((CHUNK, Z_COLS_PADDED), F32),
            pltpu.VMEM((CHUNK, d), BF16),
            pltpu.VMEM((CHUNK, d), BF16),
            pltpu.VMEM((CHUNK, d), BF16),
            pltpu.VMEM((CHUNK, d), BF16),
            pltpu.VMEM((CONV_HALO + CHUNK, CONV_CH), F32),
            pltpu.VMEM((QK_HALO + CHUNK, 2 * MLSTM_CH), F32),
            pltpu.VMEM((HEADS, DH, DH), F32),
            pltpu.VMEM((HEADS, 1, DH), F32),
            pltpu.VMEM((HEADS, 1, LANES), F32),
        ],
        compiler_params=pltpu.CompilerParams(
            dimension_semantics=("arbitrary",),
            vmem_limit_bytes=MIXER_VMEM_LIMIT),
        name="mixer_layer",
    )(x2, x2, *args)


def _mlp_kernel(x_ref, g_ref, wu_ref, wd_ref, fg_ref, o_ref, xn_ref, *, final_norm):
    k = pl.program_id(1)

    @pl.when(k == 0)
    def _():
        x = x_ref[...]
        xn_ref[...] = _rms(x, g_ref[...]).astype(BF16)
        o_ref[...] = x

    hm = jnp.dot(xn_ref[...], wu_ref[...], preferred_element_type=F32)
    hm = jnp.maximum(hm, 0.0)
    o_ref[...] += jnp.dot((hm * hm).astype(BF16), wd_ref[...], preferred_element_type=F32)

    if final_norm:
        @pl.when(k == pl.num_programs(1) - 1)
        def _():
            o_ref[...] = _rms(o_ref[...], fg_ref[...])


def _mlp(x2, layer, g, wu, wd, fg, *, final_norm, tm=1024, tf=512):
    m, d = x2.shape
    f = wu.shape[2]
    return pl.pallas_call(
        functools.partial(_mlp_kernel, final_norm=final_norm),
        grid=(m // tm, f // tf),
        in_specs=[
            pl.BlockSpec((tm, d), lambda i, k: (i, 0)),
            pl.BlockSpec((None, 1, d), lambda i, k: (layer, 0, 0)),
            pl.BlockSpec((None, d, tf), lambda i, k: (layer, 0, k)),
            pl.BlockSpec((None, tf, d), lambda i, k: (layer, k, 0)),
            pl.BlockSpec((1, d), lambda i, k: (0, 0)),
        ],
        out_specs=pl.BlockSpec((tm, d), lambda i, k: (i, 0)),
        out_shape=jax.ShapeDtypeStruct((m, d), F32),
        scratch_shapes=[pltpu.VMEM((tm, d), BF16)],
        compiler_params=pltpu.CompilerParams(
            dimension_semantics=("arbitrary", "arbitrary"),
            vmem_limit_bytes=MLP_VMEM_LIMIT),
        name="mlp",
    )(x2, g, wu, wd, fg)


def kernel(x, ln_mix_g, w_in, conv_w, conv_b, conv_norm_g, conv_norm_b, qk_conv_w, qk_conv_b,
           igate_b, fgate_b, mlstm_norm_g, gm_norm_g, gm_norm_b, gm_w, gm_b, w_out,
           ln_mlp_g, w_up, w_down, final_g):
    batch, seq, d = x.shape
    depth = w_in.shape[0]
    x2 = x.reshape(batch * seq, d)
    rows = lambda a: a.reshape(depth, 1, -1)
    gate_bias = jnp.pad(jnp.concatenate([igate_b, fgate_b], axis=-1),
                        ((0, 0), (0, LANES - 2 * HEADS)))
    w_in_b = jnp.concatenate(
        [w_in.astype(BF16), jnp.zeros(w_in.shape[:2] + (Z_COLS_PADDED - Z_COLS,), BF16)], axis=-1)
    params = dict(
        ln_mix_g=rows(ln_mix_g), w_in=w_in_b, conv_w=conv_w, conv_b=rows(conv_b),
        conv_norm_g=rows(conv_norm_g), conv_norm_b=rows(conv_norm_b), qk_conv_w=qk_conv_w,
        qk_conv_b=rows(qk_conv_b), gate_bias=rows(gate_bias), mlstm_norm_g=rows(mlstm_norm_g),
        gm_norm_g=rows(gm_norm_g), gm_norm_b=rows(gm_norm_b), gm_w=gm_w,
        gm_bt=jnp.swapaxes(gm_b, 1, 2), w_out=w_out.astype(BF16))
    w_up_b = w_up.astype(BF16)
    w_down_b = w_down.astype(BF16)
    for l in range(depth):
        x2 = _mixer_layer(x2, l, params, seq=seq)
        x2 = _mlp(x2, l, rows(ln_mlp_g), w_up_b, w_down_b, final_g.reshape(1, -1),
                  final_norm=(l == depth - 1))
    return x2.reshape(batch, seq, d)
```

```python
import functools

import jax
import jax.numpy as jnp
from jax import lax
from jax.experimental import pallas as pl
from jax.experimental.pallas import tpu as pltpu

F32 = jnp.float32
BF16 = jnp.bfloat16

EPS = 1e-6
CONV_CH = 512
CONV_WIDTH = 31
HEADS = 4
DH = 256
MLSTM_CH = HEADS * DH
QK_WIDTH = 4
CHUNK = 128
GM_GROUPS = 4
GM_CH = 512
GM_GROUP_CH = GM_CH // GM_GROUPS
LANES = 128
SUBLANES = 8

Z_CV = 0
Z_CG = Z_CV + CONV_CH
Z_Q = Z_CG + CONV_CH
Z_K = Z_Q + MLSTM_CH
Z_V = Z_K + MLSTM_CH
Z_O = Z_V + MLSTM_CH
Z_GATE = Z_O + MLSTM_CH
Z_GU = Z_GATE + 2 * HEADS
Z_GV = Z_GU + GM_CH
Z_COLS = Z_GV + GM_CH
Z_COLS_PADDED = -(-Z_COLS // LANES) * LANES
Z_COL_BLOCK = 512
MIX_A = 0
MIX_B = CONV_CH
MIX_C = CONV_CH + MLSTM_CH

CONV_HALO = 32
QK_HALO = 8

MLP_VMEM_LIMIT = 56 * 1024 * 1024
MIXER_VMEM_LIMIT = 60 * 1024 * 1024


def _rms(x, g):
    return x * lax.rsqrt(jnp.mean(x * x, axis=-1, keepdims=True) + EPS) * g


def _layernorm(x, g, b):
    mu = jnp.mean(x, axis=-1, keepdims=True)
    xc = x - mu
    var = jnp.mean(xc * xc, axis=-1, keepdims=True)
    return xc * lax.rsqrt(var + EPS) * g + b


def _gate(v, x):
    hv = 0.5 * v
    return hv * jnp.tanh(0.5 * x) + hv


def _silu(x):
    h = 0.5 * x
    return h * jnp.tanh(h) + h


def _cumsum_rows(x):
    rows = lax.broadcasted_iota(jnp.int32, x.shape, 0)
    shift = 1
    while shift < x.shape[0]:
        x = x + jnp.where(rows >= shift, pltpu.roll(x, shift, axis=0), 0.0)
        shift *= 2
    return x


def _shift_up(x, r):
    return x if r == 0 else pltpu.roll(x, x.shape[0] - r, axis=0)


def _causal_conv(buf, w_ref, b_ref, cols, *, width, halo):
    L = CHUNK
    x = buf[:, cols]
    acc = jnp.broadcast_to(b_ref[:, cols], (L, LANES))
    first = halo - (width - 1)
    for r in range(SUBLANES):
        starts = [s for s in range(first, halo + 1) if s % SUBLANES == r]
        if not starts:
            continue
        xs = _shift_up(x, r)
        for s in starts:
            k = s - first
            a0 = s - r
            acc = acc + w_ref[k:k + 1, cols] * xs[a0:a0 + L]
    return acc


def _short_conv_block(buf, w_ref, b_ref, cols):
    L = CHUNK
    n = QK_HALO + L
    x = buf[:, cols]
    x1 = pltpu.roll(x, 1, axis=0)
    w = lambda k: w_ref[k:k + 1, cols]
    u = w(3) * x + w(2) * x1
    v = w(1) * x + w(0) * x1
    y = u + pltpu.roll(v, 2, axis=0)
    assert QK_WIDTH == 4 and QK_HALO >= 3 and n % SUBLANES == 0
    return b_ref[:, cols] + y[QK_HALO:n]


def _project_tasks(x_rows, g_ref, win_ref, xn_ref, z_ref):
    def norm():
        xn_ref[...] = _rms(x_rows(), g_ref[...]).astype(BF16)

    def block(c0, c1):
        def run():
            z_ref[:, c0:c1] = jnp.dot(xn_ref[...], win_ref[:, c0:c1], preferred_element_type=F32)
        return run

    starts = list(range(0, Z_GATE, Z_COL_BLOCK))
    return [norm] + [block(c0, c0 + Z_COL_BLOCK) for c0 in starts] + [block(Z_GATE, Z_COLS)]


def _out_proj_tasks(x_rows, mix_ref, wout_ref, o_rows):
    def block(c0):
        def run():
            o_rows(c0)[...] = x_rows(c0)[...] + jnp.dot(
                mix_ref[...], wout_ref[:, c0:c0 + Z_COL_BLOCK], preferred_element_type=F32)
        return run
    return [block(c0) for c0 in range(0, wout_ref.shape[1], Z_COL_BLOCK)]


def _interleave(stages, tasks):
    total = sum(MIX_STAGE_UNITS)
    done = 0.0
    issued = 0
    for cost in stages:
        done += cost
        target = round(len(tasks) * done / total)
        while issued < target:
            tasks[issued]()
            issued += 1
    for t in tasks[issued:]:
        t()


MIX_STAGE_UNITS = (5.0, 4.5, 1.5, 2.0, 1.5)


def _mix_stages(z_ref, mix_ref, prm, state):
    units = iter(MIX_STAGE_UNITS)
    yield next(units)
    (convw_ref, convb_ref, cng_ref, cnb_ref, qkw_ref, qkb_ref, gbias_ref, mng_ref,
     gng_ref, gnb_ref, gmw_ref, gmbt_ref) = prm
    abuf, qkbuf, c_ref, n_ref, m_ref = state
    L = CHUNK
    rows = lax.broadcasted_iota(jnp.int32, (L, L), 0)
    cols = lax.broadcasted_iota(jnp.int32, (L, L), 1)
    causal = cols <= rows
    lanes = lax.broadcasted_iota(jnp.int32, (L, LANES), 1)
    zc = lambda c0, n: z_ref[:, c0:c0 + n]

    abuf[CONV_HALO:CONV_HALO + L, :] = _gate(zc(Z_CV, CONV_CH), zc(Z_CG, CONV_CH))
    acc = jnp.concatenate(
        [_causal_conv(abuf, convw_ref, convb_ref, slice(c * LANES, (c + 1) * LANES),
                      width=CONV_WIDTH, halo=CONV_HALO) for c in range(CONV_CH // LANES)],
        axis=-1)
    abuf[0:CONV_HALO, :] = abuf[L:L + CONV_HALO, :]
    a = _layernorm(acc, cng_ref[...], cnb_ref[...])
    mix_ref[:, MIX_A:MIX_A + CONV_CH] = _silu(a).astype(BF16)

    gu = jax.nn.gelu(zc(Z_GU, GM_CH))
    gv = _layernorm(jax.nn.gelu(zc(Z_GV, GM_CH)), gng_ref[...], gnb_ref[...])
    for g in range(GM_GROUPS):
        sl = slice(g * GM_GROUP_CH, (g + 1) * GM_GROUP_CH)
        wc = jnp.where(causal, gmw_ref[g], 0.0).astype(BF16)
        sp = jnp.dot(wc, gv[:, sl].astype(BF16), preferred_element_type=F32)
        sp = sp + gmbt_ref[:, g:g + 1]
        c0 = MIX_C + g * GM_GROUP_CH
        mix_ref[:, c0:c0 + GM_GROUP_CH] = (gu[:, sl] * sp).astype(BF16)
    yield next(units)

    pre = zc(Z_GATE, LANES) + gbias_ref[...]
    gl = jnp.where(lanes < HEADS, pre, jax.nn.log_sigmoid(pre))
    bc = _cumsum_rows(gl)
    gl_t = gl.T
    bc_t = bc.T

    qkbuf[QK_HALO:QK_HALO + L, :] = zc(Z_Q, 2 * MLSTM_CH)

    def short_conv(col):
        y = jnp.concatenate(
            [_short_conv_block(qkbuf, qkw_ref, qkb_ref, slice(col + c * LANES, col + (c + 1) * LANES))
             for c in range(DH // LANES)], axis=-1)
        return _silu(y)

    hs = range(HEADS)
    q = [short_conv(h * DH) for h in hs]
    k = [short_conv(MLSTM_CH + h * DH) * (DH ** -0.5) for h in hs]
    qkbuf[0:QK_HALO, :] = qkbuf[L:L + QK_HALO, :]
    q_b = [q[h].astype(BF16) for h in hs]
    v_b = [zc(Z_V + h * DH, DH).astype(BF16) for h in hs]
    li_col = [gl[:, h:h + 1] for h in hs]
    b_col = [bc[:, HEADS + h:HEADS + h + 1] for h in hs]
    m_prev = [m_ref[h][:, 0:1] for h in hs]
    dmat, inter, m_row = [], [], []
    for h in hs:
        li_row = gl_t[h:h + 1, :]
        b_row = bc_t[HEADS + h:HEADS + h + 1, :]
        dmat.append(jnp.where(causal, b_col[h] - b_row + li_row, -jnp.inf))
        inter.append(b_col[h] + m_prev[h])
        m_row.append(jnp.maximum(jnp.max(dmat[h], axis=-1, keepdims=True), inter[h]))
    s = [lax.dot_general(q_b[h], k[h].astype(BF16), (((1,), (1,)), ((), ())),
                         preferred_element_type=F32) for h in hs]
    yield next(units)

    p = [jnp.exp(dmat[h] - m_row[h]) * s[h] for h in hs]
    gdec = [jnp.exp(inter[h] - m_row[h]) for h in hs]
    c_old = [c_ref[h] for h in hs]
    n_old = [n_ref[h] for h in hs]
    num = [jnp.dot(p[h].astype(BF16), v_b[h], preferred_element_type=F32)
           + gdec[h] * jnp.dot(q_b[h], c_old[h].astype(BF16), preferred_element_type=F32)
           for h in hs]
    yield next(units)

    hh = []
    for h in hs:
        den = (jnp.sum(p[h], axis=-1, keepdims=True)
               + gdec[h] * jnp.sum(q[h] * n_old[h], axis=-1, keepdims=True))
        hh.append(num[h] / jnp.maximum(jnp.abs(den), jnp.exp(-m_row[h])))
    for h in hs:
        b_last = b_col[h][L - 1:L, :]
        a_col = b_last - b_col[h] + li_col[h]
        m_new = jnp.maximum(b_last + m_prev[h], jnp.max(a_col, axis=0, keepdims=True))
        decay = jnp.exp(b_last + m_prev[h] - m_new)
        kw = k[h] * jnp.exp(a_col - m_new)
        c_ref[h] = decay * c_old[h] + jnp.dot(kw.T.astype(BF16), v_b[h],
                                              preferred_element_type=F32)
        n_ref[h] = decay * n_old[h] + jnp.sum(kw, axis=0, keepdims=True)
        m_ref[h] = jnp.broadcast_to(m_new, (1, LANES))
    yield next(units)

    for h in hs:
        hn = _rms(hh[h], mng_ref[:, h * DH:(h + 1) * DH])
        c0 = MIX_B + h * DH
        mix_ref[:, c0:c0 + DH] = _gate(hn, zc(Z_O + h * DH, DH)).astype(BF16)


def _mixer_layer_kernel(x_ref, xnext_ref, g_ref, win_ref, *rest, steps_per_seq):
    prm = rest[:12]
    wout_ref, o_ref = rest[12:14]
    za, zb, xna, xnb, mixa, mixb = rest[14:20]
    state = rest[20:]
    abuf, qkbuf, c_ref, n_ref, m_ref = state
    L = CHUNK
    s = pl.program_id(0)
    rows0 = lambda c0: (slice(0, L), slice(c0, c0 + Z_COL_BLOCK))
    rows1 = lambda c0: (slice(L, 2 * L), slice(c0, c0 + Z_COL_BLOCK))

    @pl.when(s == 0)
    def _():
        for task in _project_tasks(lambda: x_ref[0:L, :], g_ref, win_ref, xna, za):
            task()

    @pl.when(lax.rem(s, steps_per_seq) == 0)
    def _():
        abuf[0:CONV_HALO, :] = jnp.zeros((CONV_HALO, CONV_CH), F32)
        qkbuf[0:QK_HALO, :] = jnp.zeros((QK_HALO, 2 * MLSTM_CH), F32)
        c_ref[...] = jnp.zeros_like(c_ref)
        n_ref[...] = jnp.zeros_like(n_ref)
        m_ref[...] = jnp.zeros_like(m_ref)

    _interleave(_mix_stages(za, mixa, prm, state),
                _project_tasks(lambda: x_ref[L:2 * L, :], g_ref, win_ref, xnb, zb))
    _interleave(_mix_stages(zb, mixb, prm, state),
                _out_proj_tasks(lambda c0: x_ref.at[rows0(c0)], mixa, wout_ref,
                                lambda c0: o_ref.at[rows0(c0)])
                + _project_tasks(lambda: xnext_ref[...], g_ref, win_ref, xna, za))
    for task in _out_proj_tasks(lambda c0: x_ref.at[rows1(c0)], mixb, wout_ref,
                                lambda c0: o_ref.at[rows1(c0)]):
        task()


def _mixer_layer(x2, layer, p, *, seq):
    m, d = x2.shape
    t = 2 * CHUNK
    steps = m // t
    last_chunk = m // CHUNK - 1

    def stacked(a):
        return pl.BlockSpec((None,) + a.shape[1:], (lambda s: (layer,) + (0,) * (a.ndim - 1)),
                            pipeline_mode=pl.Buffered(1))

    names = ["ln_mix_g", "w_in", "conv_w", "conv_b", "conv_norm_g", "conv_norm_b", "qk_conv_w",
             "qk_conv_b", "gate_bias", "mlstm_norm_g", "gm_norm_g", "gm_norm_b", "gm_w", "gm_bt",
             "w_out"]
    args = [p[k] for k in names]
    return pl.pallas_call(
        functools.partial(_mixer_layer_kernel, steps_per_seq=seq // t),
        grid=(steps,),
        in_specs=[
            pl.BlockSpec((t, d), lambda s: (s, 0)),
            pl.BlockSpec((CHUNK, d), lambda s: (jnp.minimum(2 * s + 2, last_chunk), 0)),
            *[stacked(a) for a in args],
        ],
        out_specs=pl.BlockSpec((t, d), lambda s: (s, 0)),
        out_shape=jax.ShapeDtypeStruct((m, d), F32),
        scratch_shapes=[
            pltpu.VMEM((CHUNK, Z_COLS_PADDED), F32),
            pltpu.VMEM((CHUNK, Z_COLS_PADDED), F32),
            pltpu.VMEM((CHUNK, d), BF16),
            pltpu.VMEM((CHUNK, d), BF16),
            pltpu.VMEM((CHUNK, d), BF16),
            pltpu.VMEM((CHUNK, d), BF16),
            pltpu.VMEM((CONV_HALO + CHUNK, CONV_CH), F32),
            pltpu.VMEM((QK_HALO + CHUNK, 2 * MLSTM_CH), F32),
            pltpu.VMEM((HEADS, DH, DH), F32),
            pltpu.VMEM((HEADS, 1, DH), F32),
            pltpu.VMEM((HEADS, 1, LANES), F32),
        ],
        compiler_params=pltpu.CompilerParams(
            dimension_semantics=("arbitrary",),
            vmem_limit_bytes=MIXER_VMEM_LIMIT),
        name="mixer_layer",
    )(x2, x2, *args)


def _mlp_kernel(x_hbm, g_ref, wu_ref, wd_ref, fg_ref, o_ref, xbuf, xn_ref, sem, *,
                final_norm, tm):
    i = pl.program_id(0)
    k = pl.program_id(1)

    def x_copy(tile):
        return pltpu.make_async_copy(x_hbm.at[pl.ds(tile * tm, tm), :], xbuf, sem)

    @pl.when(jnp.logical_and(i == 0, k == 0))
    def _():
        x_copy(0).start()

    @pl.when(k == 0)
    def _():
        x_copy(i).wait()
        x = xbuf[...]
        xn_ref[...] = _rms(x, g_ref[...]).astype(BF16)
        o_ref[...] = x

    @pl.when(jnp.logical_and(k == 1, i + 1 < pl.num_programs(0)))
    def _():
        x_copy(i + 1).start()

    hm = jnp.dot(xn_ref[...], wu_ref[...], preferred_element_type=F32)
    hm = jnp.maximum(hm, 0.0)
    o_ref[...] += jnp.dot((hm * hm).astype(BF16), wd_ref[...], preferred_element_type=F32)

    if final_norm:
        @pl.when(k == pl.num_programs(1) - 1)
        def _():
            o_ref[...] = _rms(o_ref[...], fg_ref[...])


def _mlp(x2, layer, g, wu, wd, fg, *, final_norm, tm=1024, tf=1024):
    m, d = x2.shape
    f = wu.shape[2]
    assert f // tf >= 2
    return pl.pallas_call(
        functools.partial(_mlp_kernel, final_norm=final_norm, tm=tm),
        grid=(m // tm, f // tf),
        in_specs=[
            pl.BlockSpec(memory_space=pl.ANY),
            pl.BlockSpec((None, 1, d), lambda i, k: (layer, 0, 0)),
            pl.BlockSpec((None, d, tf), lambda i, k: (layer, 0, k)),
            pl.BlockSpec((None, tf, d), lambda i, k: (layer, k, 0)),
            pl.BlockSpec((1, d), lambda i, k: (0, 0)),
        ],
        out_specs=pl.BlockSpec((tm, d), lambda i, k: (i, 0)),
        out_shape=jax.ShapeDtypeStruct((m, d), F32),
        scratch_shapes=[pltpu.VMEM((tm, d), F32), pltpu.VMEM((tm, d), BF16),
                        pltpu.SemaphoreType.DMA(())],
        compiler_params=pltpu.CompilerParams(
            dimension_semantics=("arbitrary", "arbitrary"),
            vmem_limit_bytes=MLP_VMEM_LIMIT),
        name="mlp",
    )(x2, g, wu, wd, fg)


def kernel(x, ln_mix_g, w_in, conv_w, conv_b, conv_norm_g, conv_norm_b, qk_conv_w, qk_conv_b,
           igate_b, fgate_b, mlstm_norm_g, gm_norm_g, gm_norm_b, gm_w, gm_b, w_out,
           ln_mlp_g, w_up, w_down, final_g):
    batch, seq, d = x.shape
    depth = w_in.shape[0]
    x2 = x.reshape(batch * seq, d)
    rows = lambda a: a.reshape(depth, 1, -1)
    gate_bias = jnp.pad(jnp.concatenate([igate_b, fgate_b], axis=-1),
                        ((0, 0), (0, LANES - 2 * HEADS)))
    w_in_b = jnp.concatenate(
        [w_in.astype(BF16), jnp.zeros(w_in.shape[:2] + (Z_COLS_PADDED - Z_COLS,), BF16)], axis=-1)
    params = dict(
        ln_mix_g=rows(ln_mix_g), w_in=w_in_b, conv_w=conv_w, conv_b=rows(conv_b),
        conv_norm_g=rows(conv_norm_g), conv_norm_b=rows(conv_norm_b), qk_conv_w=qk_conv_w,
        qk_conv_b=rows(qk_conv_b), gate_bias=rows(gate_bias), mlstm_norm_g=rows(mlstm_norm_g),
        gm_norm_g=rows(gm_norm_g), gm_norm_b=rows(gm_norm_b), gm_w=gm_w,
        gm_bt=jnp.swapaxes(gm_b, 1, 2), w_out=w_out.astype(BF16))
    w_up_b = w_up.astype(BF16)
    w_down_b = w_down.astype(BF16)
    for l in range(depth):
        x2 = _mixer_layer(x2, l, params, seq=seq)
        x2 = _mlp(x2, l, rows(ln_mlp_g), w_up_b, w_down_b, final_g.reshape(1, -1),
                  final_norm=(l == depth - 1))
    return x2.reshape(batch, seq, d)
```

```python
import functools

import jax
import jax.numpy as jnp
from jax import lax
from jax.experimental import pallas as pl
from jax.experimental.pallas import tpu as pltpu

F32 = jnp.float32
BF16 = jnp.bfloat16

EPS = 1e-6
CONV_CH = 512
CONV_WIDTH = 31
HEADS = 4
DH = 256
MLSTM_CH = HEADS * DH
QK_WIDTH = 4
CHUNK = 128
GM_GROUPS = 4
GM_CH = 512
GM_GROUP_CH = GM_CH // GM_GROUPS
LANES = 128
SUBLANES = 8

Z_CV = 0
Z_CG = Z_CV + CONV_CH
Z_Q = Z_CG + CONV_CH
Z_K = Z_Q + MLSTM_CH
Z_V = Z_K + MLSTM_CH
Z_O = Z_V + MLSTM_CH
Z_GATE = Z_O + MLSTM_CH
Z_GU = Z_GATE + 2 * HEADS
Z_GV = Z_GU + GM_CH
Z_COLS = Z_GV + GM_CH
Z_COLS_PADDED = -(-Z_COLS // LANES) * LANES
Z_COL_BLOCK = 512
MIX_A = 0
MIX_B = CONV_CH
MIX_C = CONV_CH + MLSTM_CH

CONV_HALO = 32
QK_HALO = 8

MLP_VMEM_LIMIT = 56 * 1024 * 1024
MIXER_VMEM_LIMIT = 60 * 1024 * 1024


def _rms(x, g):
    return x * lax.rsqrt(jnp.mean(x * x, axis=-1, keepdims=True) + EPS) * g


def _layernorm(x, g, b):
    mu = jnp.mean(x, axis=-1, keepdims=True)
    xc = x - mu
    var = jnp.mean(xc * xc, axis=-1, keepdims=True)
    return xc * lax.rsqrt(var + EPS) * g + b


def _gate(v, x):
    hv = 0.5 * v
    return hv * jnp.tanh(0.5 * x) + hv


def _silu(x):
    h = 0.5 * x
    return h * jnp.tanh(h) + h


def _cumsum_rows(x):
    rows = lax.broadcasted_iota(jnp.int32, x.shape, 0)
    shift = 1
    while shift < x.shape[0]:
        x = x + jnp.where(rows >= shift, pltpu.roll(x, shift, axis=0), 0.0)
        shift *= 2
    return x


def _shift_up(x, r):
    return x if r == 0 else pltpu.roll(x, x.shape[0] - r, axis=0)


def _causal_conv(buf, w_ref, b_ref, cols, *, width, halo):
    L = CHUNK
    x = buf[:, cols]
    acc = jnp.broadcast_to(b_ref[:, cols], (L, LANES))
    first = halo - (width - 1)
    for r in range(SUBLANES):
        starts = [s for s in range(first, halo + 1) if s % SUBLANES == r]
        if not starts:
            continue
        xs = _shift_up(x, r)
        for s in starts:
            k = s - first
            a0 = s - r
            acc = acc + w_ref[k:k + 1, cols] * xs[a0:a0 + L]
    return acc


def _short_conv_block(buf, w_ref, b_ref, cols):
    L = CHUNK
    n = QK_HALO + L
    x = buf[:, cols]
    x1 = pltpu.roll(x, 1, axis=0)
    w = lambda k: w_ref[k:k + 1, cols]
    u = w(3) * x + w(2) * x1
    v = w(1) * x + w(0) * x1
    y = u + pltpu.roll(v, 2, axis=0)
    assert QK_WIDTH == 4 and QK_HALO >= 3 and n % SUBLANES == 0
    return b_ref[:, cols] + y[QK_HALO:n]


def _project_tasks(x_rows, g_ref, win_ref, xn_ref, z_ref):
    def norm():
        xn_ref[...] = _rms(x_rows(), g_ref[...]).astype(BF16)

    def block(c0, c1):
        def run():
            z_ref[:, c0:c1] = jnp.dot(xn_ref[...], win_ref[:, c0:c1], preferred_element_type=F32)
        return run

    starts = list(range(0, Z_GATE, Z_COL_BLOCK))
    return [norm] + [block(c0, c0 + Z_COL_BLOCK) for c0 in starts] + [block(Z_GATE, Z_COLS)]


def _out_proj_tasks(x_rows, mix_ref, wout_ref, o_rows):
    def block(c0):
        def run():
            o_rows(c0)[...] = x_rows(c0)[...] + jnp.dot(
                mix_ref[...], wout_ref[:, c0:c0 + Z_COL_BLOCK], preferred_element_type=F32)
        return run
    return [block(c0) for c0 in range(0, mix_ref.shape[1], Z_COL_BLOCK)]


def _interleave(stages, tasks):
    total = sum(MIX_STAGE_UNITS)
    done = 0.0
    issued = 0
    for cost in stages:
        done += cost
        target = round(len(tasks) * done / total)
        while issued < target:
            tasks[issued]()
            issued += 1
    for t in tasks[issued:]:
        t()


MIX_STAGE_UNITS = (5.0, 4.5, 1.5, 2.0, 1.5)


def _mix_stages(z_ref, mix_ref, prm, state):
    units = iter(MIX_STAGE_UNITS)
    yield next(units)
    (convw_ref, convb_ref, cng_ref, cnb_ref, qkw_ref, qkb_ref, gbias_ref, mng_ref,
     gng_ref, gnb_ref, gmw_ref, gmbt_ref) = prm
    abuf, qkbuf, c_ref, n_ref, m_ref = state
    L = CHUNK
    rows = lax.broadcasted_iota(jnp.int32, (L, L), 0)
    cols = lax.broadcasted_iota(jnp.int32, (L, L), 1)
    causal = cols <= rows
    lanes = lax.broadcasted_iota(jnp.int32, (L, LANES), 1)
    zc = lambda c0, n: z_ref[:, c0:c0 + n]

    abuf[CONV_HALO:CONV_HALO + L, :] = _gate(zc(Z_CV, CONV_CH), zc(Z_CG, CONV_CH))
    acc = jnp.concatenate(
        [_causal_conv(abuf, convw_ref, convb_ref, slice(c * LANES, (c + 1) * LANES),
                      width=CONV_WIDTH, halo=CONV_HALO) for c in range(CONV_CH // LANES)],
        axis=-1)
    abuf[0:CONV_HALO, :] = abuf[L:L + CONV_HALO, :]
    a = _layernorm(acc, cng_ref[...], cnb_ref[...])
    mix_ref[:, MIX_A:MIX_A + CONV_CH] = _silu(a).astype(BF16)

    gu = jax.nn.gelu(zc(Z_GU, GM_CH))
    gv = _layernorm(jax.nn.gelu(zc(Z_GV, GM_CH)), gng_ref[...], gnb_ref[...])
    for g in range(GM_GROUPS):
        sl = slice(g * GM_GROUP_CH, (g + 1) * GM_GROUP_CH)
        wc = jnp.where(causal, gmw_ref[g], 0.0).astype(BF16)
        sp = jnp.dot(wc, gv[:, sl].astype(BF16), preferred_element_type=F32)
        sp = sp + gmbt_ref[:, g:g + 1]
        c0 = MIX_C + g * GM_GROUP_CH
        mix_ref[:, c0:c0 + GM_GROUP_CH] = (gu[:, sl] * sp).astype(BF16)
    yield next(units)

    pre = zc(Z_GATE, LANES) + gbias_ref[...]
    gl = jnp.where(lanes < HEADS, pre, jax.nn.log_sigmoid(pre))
    bc = _cumsum_rows(gl)
    gl_t = gl.T
    bc_t = bc.T

    qkbuf[QK_HALO:QK_HALO + L, :] = zc(Z_Q, 2 * MLSTM_CH)

    def short_conv(col):
        y = jnp.concatenate(
            [_short_conv_block(qkbuf, qkw_ref, qkb_ref, slice(col + c * LANES, col + (c + 1) * LANES))
             for c in range(DH // LANES)], axis=-1)
        return _silu(y)

    hs = range(HEADS)
    q = [short_conv(h * DH) for h in hs]
    k = [short_conv(MLSTM_CH + h * DH) * (DH ** -0.5) for h in hs]
    qkbuf[0:QK_HALO, :] = qkbuf[L:L + QK_HALO, :]
    q_b = [q[h].astype(BF16) for h in hs]
    v_b = [zc(Z_V + h * DH, DH).astype(BF16) for h in hs]
    li_col = [gl[:, h:h + 1] for h in hs]
    b_col = [bc[:, HEADS + h:HEADS + h + 1] for h in hs]
    m_prev = [m_ref[h][:, 0:1] for h in hs]
    dmat, inter, m_row = [], [], []
    for h in hs:
        li_row = gl_t[h:h + 1, :]
        b_row = bc_t[HEADS + h:HEADS + h + 1, :]
        dmat.append(jnp.where(causal, b_col[h] - b_row + li_row, -jnp.inf))
        inter.append(b_col[h] + m_prev[h])
        m_row.append(jnp.maximum(jnp.max(dmat[h], axis=-1, keepdims=True), inter[h]))
    s = [lax.dot_general(q_b[h], k[h].astype(BF16), (((1,), (1,)), ((), ())),
                         preferred_element_type=F32) for h in hs]
    yield next(units)

    p = [jnp.exp(dmat[h] - m_row[h]) * s[h] for h in hs]
    gdec = [jnp.exp(inter[h] - m_row[h]) for h in hs]
    c_old = [c_ref[h] for h in hs]
    n_old = [n_ref[h] for h in hs]
    num = [jnp.dot(p[h].astype(BF16), v_b[h], preferred_element_type=F32)
           + gdec[h] * jnp.dot(q_b[h], c_old[h].astype(BF16), preferred_element_type=F32)
           for h in hs]
    yield next(units)

    hh = []
    for h in hs:
        den = (jnp.sum(p[h], axis=-1, keepdims=True)
               + gdec[h] * jnp.sum(q[h] * n_old[h], axis=-1, keepdims=True))
        hh.append(num[h] / jnp.maximum(jnp.abs(den), jnp.exp(-m_row[h])))
    for h in hs:
        b_last = b_col[h][L - 1:L, :]
        a_col = b_last - b_col[h] + li_col[h]
        m_new = jnp.maximum(b_last + m_prev[h], jnp.max(a_col, axis=0, keepdims=True))
        decay = jnp.exp(b_last + m_prev[h] - m_new)
        kw = k[h] * jnp.exp(a_col - m_new)
        c_ref[h] = decay * c_old[h] + jnp.dot(kw.T.astype(BF16), v_b[h],
                                              preferred_element_type=F32)
        n_ref[h] = decay * n_old[h] + jnp.sum(kw, axis=0, keepdims=True)
        m_ref[h] = jnp.broadcast_to(m_new, (1, LANES))
    yield next(units)

    for h in hs:
        hn = _rms(hh[h], mng_ref[:, h * DH:(h + 1) * DH])
        c0 = MIX_B + h * DH
        mix_ref[:, c0:c0 + DH] = _gate(hn, zc(Z_O + h * DH, DH)).astype(BF16)


def _mixer_layer_kernel(x_ref, xnext_ref, g_ref, win_ref, *rest, steps_per_seq):
    prm = rest[:12]
    wout_ref, o_ref = rest[12:14]
    za, zb, xna, xnb, mixa, mixb = rest[14:20]
    state = rest[20:]
    abuf, qkbuf, c_ref, n_ref, m_ref = state
    L = CHUNK
    s = pl.program_id(0)
    rows0 = lambda c0: (slice(0, L), slice(c0, c0 + Z_COL_BLOCK))
    rows1 = lambda c0: (slice(L, 2 * L), slice(c0, c0 + Z_COL_BLOCK))

    @pl.when(s == 0)
    def _():
        for task in _project_tasks(lambda: x_ref[0:L, :], g_ref, win_ref, xna, za):
            task()

    @pl.when(lax.rem(s, steps_per_seq) == 0)
    def _():
        abuf[0:CONV_HALO, :] = jnp.zeros((CONV_HALO, CONV_CH), F32)
        qkbuf[0:QK_HALO, :] = jnp.zeros((QK_HALO, 2 * MLSTM_CH), F32)
        c_ref[...] = jnp.zeros_like(c_ref)
        n_ref[...] = jnp.zeros_like(n_ref)
        m_ref[...] = jnp.zeros_like(m_ref)

    _interleave(_mix_stages(za, mixa, prm, state),
                _project_tasks(lambda: x_ref[L:2 * L, :], g_ref, win_ref, xnb, zb))
    _interleave(_mix_stages(zb, mixb, prm, state),
                _out_proj_tasks(lambda c0: x_ref.at[rows0(c0)], mixa, wout_ref,
                                lambda c0: o_ref.at[rows0(c0)])
                + _project_tasks(lambda: xnext_ref[...], g_ref, win_ref, xna, za))
    for task in _out_proj_tasks(lambda c0: x_ref.at[rows1(c0)], mixb, wout_ref,
                                lambda c0: o_ref.at[rows1(c0)]):
        task()


def _mixer_layer(x2, layer, p, *, seq):
    m, d = x2.shape
    t = 2 * CHUNK
    steps = m // t
    last_chunk = m // CHUNK - 1

    def stacked(a):
        return pl.BlockSpec((None,) + a.shape[1:], (lambda s: (layer,) + (0,) * (a.ndim - 1)),
                            pipeline_mode=pl.Buffered(1))

    names = ["ln_mix_g", "w_in", "conv_w", "conv_b", "conv_norm_g", "conv_norm_b", "qk_conv_w",
             "qk_conv_b", "gate_bias", "mlstm_norm_g", "gm_norm_g", "gm_norm_b", "gm_w", "gm_bt",
             "w_out"]
    args = [p[k] for k in names]
    return pl.pallas_call(
        functools.partial(_mixer_layer_kernel, steps_per_seq=seq // t),
        grid=(steps,),
        in_specs=[
            pl.BlockSpec((t, d), lambda s: (s, 0)),
            pl.BlockSpec((CHUNK, d), lambda s: (jnp.minimum(2 * s + 2, last_chunk), 0)),
            *[stacked(a) for a in args],
        ],
        out_specs=pl.BlockSpec((t, d), lambda s: (s, 0)),
        out_shape=jax.ShapeDtypeStruct((m, d), F32),
        scratch_shapes=[
            pltpu.VMEM((CHUNK, Z_COLS_PADDED), F32),
            pltpu.VMEM((CHUNK, Z_COLS_PADDED), F32),
            pltpu.VMEM((CHUNK, d), BF16),
            pltpu.VMEM((CHUNK, d), BF16),
            pltpu.VMEM((CHUNK, d), BF16),
            pltpu.VMEM((CHUNK, d), BF16),
            pltpu.VMEM((CONV_HALO + CHUNK, CONV_CH), F32),
            pltpu.VMEM((QK_HALO + CHUNK, 2 * MLSTM_CH), F32),
            pltpu.VMEM((HEADS, DH, DH), F32),
            pltpu.VMEM((HEADS, 1, DH), F32),
            pltpu.VMEM((HEADS, 1, LANES), F32),
        ],
        compiler_params=pltpu.CompilerParams(
            dimension_semantics=("arbitrary",),
            vmem_limit_bytes=MIXER_VMEM_LIMIT),
        name="mixer_layer",
    )(x2, x2, *args)


def _mlp_kernel(x_hbm, g_ref, wu_ref, wd_ref, fg_ref, o_ref, xbuf, xn_ref, sem, *,
                final_norm, tm):
    i = pl.program_id(0)
    k = pl.program_id(1)

    def x_copy(tile):
        return pltpu.make_async_copy(x_hbm.at[pl.ds(tile * tm, tm), :], xbuf, sem)

    @pl.when(jnp.logical_and(i == 0, k == 0))
    def _():
        x_copy(0).start()

    @pl.when(k == 0)
    def _():
        x_copy(i).wait()
        x = xbuf[...]
        xn_ref[...] = _rms(x, g_ref[...]).astype(BF16)
        o_ref[...] = x

    @pl.when(jnp.logical_and(k == 1, i + 1 < pl.num_programs(0)))
    def _():
        x_copy(i + 1).start()

    hm = jnp.dot(xn_ref[...], wu_ref[...], preferred_element_type=F32)
    hm = jnp.maximum(hm, 0.0)
    o_ref[...] += jnp.dot((hm * hm).astype(BF16), wd_ref[...], preferred_element_type=F32)

    if final_norm:
        @pl.when(k == pl.num_programs(1) - 1)
        def _():
            o_ref[...] = _rms(o_ref[...], fg_ref[...])


def _mlp(x2, layer, g, wu, wd, fg, *, final_norm, tm=1024, tf=1024):
    m, d = x2.shape
    f = wu.shape[2]
    assert f // tf >= 2
    return pl.pallas_call(
        functools.partial(_mlp_kernel, final_norm=final_norm, tm=tm),
        grid=(m // tm, f // tf),
        in_specs=[
            pl.BlockSpec(memory_space=pl.ANY),
            pl.BlockSpec((None, 1, d), lambda i, k: (layer, 0, 0)),
            pl.BlockSpec((None, d, tf), lambda i, k: (layer, 0, k)),
            pl.BlockSpec((None, tf, d), lambda i, k: (layer, k, 0)),
            pl.BlockSpec((1, d), lambda i, k: (0, 0)),
        ],
        out_specs=pl.BlockSpec((tm, d), lambda i, k: (i, 0)),
        out_shape=jax.ShapeDtypeStruct((m, d), F32),
        scratch_shapes=[pltpu.VMEM((tm, d), F32), pltpu.VMEM((tm, d), BF16),
                        pltpu.SemaphoreType.DMA(())],
        compiler_params=pltpu.CompilerParams(
            dimension_semantics=("arbitrary", "arbitrary"),
            vmem_limit_bytes=MLP_VMEM_LIMIT),
        name="mlp",
    )(x2, g, wu, wd, fg)


def kernel(x, ln_mix_g, w_in, conv_w, conv_b, conv_norm_g, conv_norm_b, qk_conv_w, qk_conv_b,
           igate_b, fgate_b, mlstm_norm_g, gm_norm_g, gm_norm_b, gm_w, gm_b, w_out,
           ln_mlp_g, w_up, w_down, final_g):
    batch, seq, d = x.shape
    depth = w_in.shape[0]
    x2 = x.reshape(batch * seq, d)
    rows = lambda a: a.reshape(depth, 1, -1)
    gate_bias = jnp.pad(jnp.concatenate([igate_b, fgate_b], axis=-1),
                        ((0, 0), (0, LANES - 2 * HEADS)))
    w_in_b = jnp.concatenate(
        [w_in.astype(BF16), jnp.zeros(w_in.shape[:2] + (Z_COLS_PADDED - Z_COLS,), BF16)], axis=-1)
    params = dict(
        ln_mix_g=rows(ln_mix_g), w_in=w_in_b, conv_w=conv_w, conv_b=rows(conv_b),
        conv_norm_g=rows(conv_norm_g), conv_norm_b=rows(conv_norm_b), qk_conv_w=qk_conv_w,
        qk_conv_b=rows(qk_conv_b), gate_bias=rows(gate_bias), mlstm_norm_g=rows(mlstm_norm_g),
        gm_norm_g=rows(gm_norm_g), gm_norm_b=rows(gm_norm_b), gm_w=gm_w,
        gm_bt=jnp.swapaxes(gm_b, 1, 2),
        w_out=jnp.pad(w_out, ((0, 0), (0, 0), (0, LANES))).astype(BF16))
    w_up_b = w_up.astype(BF16)
    w_down_b = w_down.astype(BF16)
    for l in range(depth):
        x2 = _mixer_layer(x2, l, params, seq=seq)
        x2 = _mlp(x2, l, rows(ln_mlp_g), w_up_b, w_down_b, final_g.reshape(1, -1),
                  final_norm=(l == depth - 1))
    return x2.reshape(batch, seq, d)
```

```python
import functools

import jax
import jax.numpy as jnp
from jax import lax
from jax.experimental import pallas as pl
from jax.experimental.pallas import tpu as pltpu

F32 = jnp.float32
BF16 = jnp.bfloat16

EPS = 1e-6
CONV_CH = 512
CONV_WIDTH = 31
HEADS = 4
DH = 256
MLSTM_CH = HEADS * DH
QK_WIDTH = 4
CHUNK = 128
GM_GROUPS = 4
GM_CH = 512
GM_GROUP_CH = GM_CH // GM_GROUPS
LANES = 128
SUBLANES = 8

Z_CV = 0
Z_CG = Z_CV + CONV_CH
Z_Q = Z_CG + CONV_CH
Z_K = Z_Q + MLSTM_CH
Z_V = Z_K + MLSTM_CH
Z_O = Z_V + MLSTM_CH
Z_GATE = Z_O + MLSTM_CH
Z_GU = Z_GATE + 2 * HEADS
Z_GV = Z_GU + GM_CH
Z_COLS = Z_GV + GM_CH
Z_COLS_PADDED = -(-Z_COLS // LANES) * LANES
Z_COL_BLOCK = 512
MIX_A = 0
MIX_B = CONV_CH
MIX_C = CONV_CH + MLSTM_CH

CONV_HALO = 32
QK_HALO = 8

MLP_VMEM_LIMIT = 56 * 1024 * 1024
MIXER_VMEM_LIMIT = 60 * 1024 * 1024


def _rms(x, g):
    return x * lax.rsqrt(jnp.mean(x * x, axis=-1, keepdims=True) + EPS) * g


def _layernorm(x, g, b):
    mu = jnp.mean(x, axis=-1, keepdims=True)
    xc = x - mu
    var = jnp.mean(xc * xc, axis=-1, keepdims=True)
    return xc * lax.rsqrt(var + EPS) * g + b


def _gate(v, x):
    hv = 0.5 * v
    return hv * jnp.tanh(0.5 * x) + hv


def _silu(x):
    h = 0.5 * x
    return h * jnp.tanh(h) + h


def _cumsum_rows(x):
    rows = lax.broadcasted_iota(jnp.int32, x.shape, 0)
    shift = 1
    while shift < x.shape[0]:
        x = x + jnp.where(rows >= shift, pltpu.roll(x, shift, axis=0), 0.0)
        shift *= 2
    return x


def _shift_up(x, r):
    return x if r == 0 else pltpu.roll(x, x.shape[0] - r, axis=0)


def _causal_conv(buf, w_ref, b_ref, cols, *, width, halo):
    L = CHUNK
    x = buf[:, cols]
    acc = jnp.broadcast_to(b_ref[:, cols], (L, LANES))
    first = halo - (width - 1)
    for r in range(SUBLANES):
        starts = [s for s in range(first, halo + 1) if s % SUBLANES == r]
        if not starts:
            continue
        xs = _shift_up(x, r)
        for s in starts:
            k = s - first
            a0 = s - r
            acc = acc + w_ref[k:k + 1, cols] * xs[a0:a0 + L]
    return acc


def _short_conv_block(buf, w_ref, b_ref, cols):
    L = CHUNK
    n = QK_HALO + L
    x = buf[:, cols]
    x1 = pltpu.roll(x, 1, axis=0)
    w = lambda k: w_ref[k:k + 1, cols]
    u = w(3) * x + w(2) * x1
    v = w(1) * x + w(0) * x1
    y = u + pltpu.roll(v, 2, axis=0)
    assert QK_WIDTH == 4 and QK_HALO >= 3 and n % SUBLANES == 0
    return b_ref[:, cols] + y[QK_HALO:n]


def _project_tasks(x_rows, g_ref, win_ref, xn_ref, z_ref):
    def norm():
        xn_ref[...] = _rms(x_rows(), g_ref[...]).astype(BF16)

    def block(c0, c1):
        def run():
            z_ref[:, c0:c1] = jnp.dot(xn_ref[...], win_ref[:, c0:c1], preferred_element_type=F32)
        return run

    starts = list(range(0, Z_GATE, Z_COL_BLOCK))
    return [norm] + [block(c0, c0 + Z_COL_BLOCK) for c0 in starts] + [block(Z_GATE, Z_COLS)]


def _out_proj_tasks(x_rows, mix_ref, wout_ref, o_rows):
    def block(c0):
        def run():
            o_rows(c0)[...] = x_rows(c0)[...] + jnp.dot(
                mix_ref[...], wout_ref[:, c0:c0 + Z_COL_BLOCK], preferred_element_type=F32)
        return run
    return [block(c0) for c0 in range(0, wout_ref.shape[1], Z_COL_BLOCK)]


def _interleave(stages, tasks):
    total = sum(MIX_STAGE_UNITS)
    done = 0.0
    issued = 0
    for cost in stages:
        done += cost
        target = round(len(tasks) * done / total)
        while issued < target:
            tasks[issued]()
            issued += 1
    for t in tasks[issued:]:
        t()


MIX_STAGE_UNITS = (5.0, 4.5, 1.5, 2.0, 1.5)


def _mix_stages(z_ref, mix_ref, prm, state):
    units = iter(MIX_STAGE_UNITS)
    yield next(units)
    (convw_ref, convb_ref, cng_ref, cnb_ref, qkw_ref, qkb_ref, gbias_ref, mng_ref,
     gng_ref, gnb_ref, gmw_ref, gmbt_ref) = prm
    abuf, qkbuf, c_ref, n_ref, m_ref = state
    L = CHUNK
    rows = lax.broadcasted_iota(jnp.int32, (L, L), 0)
    cols = lax.broadcasted_iota(jnp.int32, (L, L), 1)
    causal = cols <= rows
    lanes = lax.broadcasted_iota(jnp.int32, (L, LANES), 1)
    zc = lambda c0, n: z_ref[:, c0:c0 + n]

    abuf[CONV_HALO:CONV_HALO + L, :] = _gate(zc(Z_CV, CONV_CH), zc(Z_CG, CONV_CH))
    acc = jnp.concatenate(
        [_causal_conv(abuf, convw_ref, convb_ref, slice(c * LANES, (c + 1) * LANES),
                      width=CONV_WIDTH, halo=CONV_HALO) for c in range(CONV_CH // LANES)],
        axis=-1)
    abuf[0:CONV_HALO, :] = abuf[L:L + CONV_HALO, :]
    a = _layernorm(acc, cng_ref[...], cnb_ref[...])
    mix_ref[:, MIX_A:MIX_A + CONV_CH] = _silu(a).astype(BF16)

    gu = jax.nn.gelu(zc(Z_GU, GM_CH))
    gv = _layernorm(jax.nn.gelu(zc(Z_GV, GM_CH)), gng_ref[...], gnb_ref[...])
    for g in range(GM_GROUPS):
        sl = slice(g * GM_GROUP_CH, (g + 1) * GM_GROUP_CH)
        wc = jnp.where(causal, gmw_ref[g], 0.0).astype(BF16)
        sp = jnp.dot(wc, gv[:, sl].astype(BF16), preferred_element_type=F32)
        sp = sp + gmbt_ref[:, g:g + 1]
        c0 = MIX_C + g * GM_GROUP_CH
        mix_ref[:, c0:c0 + GM_GROUP_CH] = (gu[:, sl] * sp).astype(BF16)
    yield next(units)

    pre = zc(Z_GATE, LANES) + gbias_ref[...]
    gl = jnp.where(lanes < HEADS, pre, jax.nn.log_sigmoid(pre))
    bc = _cumsum_rows(gl)
    gl_t = gl.T
    bc_t = bc.T

    qkbuf[QK_HALO:QK_HALO + L, :] = zc(Z_Q, 2 * MLSTM_CH)

    def short_conv(col):
        y = jnp.concatenate(
            [_short_conv_block(qkbuf, qkw_ref, qkb_ref, slice(col + c * LANES, col + (c + 1) * LANES))
             for c in range(DH // LANES)], axis=-1)
        return _silu(y)

    hs = range(HEADS)
    q = [short_conv(h * DH) for h in hs]
    k = [short_conv(MLSTM_CH + h * DH) * (DH ** -0.5) for h in hs]
    qkbuf[0:QK_HALO, :] = qkbuf[L:L + QK_HALO, :]
    q_b = [q[h].astype(BF16) for h in hs]
    v_b = [zc(Z_V + h * DH, DH).astype(BF16) for h in hs]
    li_col = [gl[:, h:h + 1] for h in hs]
    b_col = [bc[:, HEADS + h:HEADS + h + 1] for h in hs]
    m_prev = [m_ref[h][:, 0:1] for h in hs]
    dmat, inter, m_row = [], [], []
    for h in hs:
        li_row = gl_t[h:h + 1, :]
        b_row = bc_t[HEADS + h:HEADS + h + 1, :]
        dmat.append(jnp.where(causal, b_col[h] - b_row + li_row, -jnp.inf))
        inter.append(b_col[h] + m_prev[h])
        m_row.append(jnp.maximum(jnp.max(dmat[h], axis=-1, keepdims=True), inter[h]))
    s = [lax.dot_general(q_b[h], k[h].astype(BF16), (((1,), (1,)), ((), ())),
                         preferred_element_type=F32) for h in hs]
    yield next(units)

    p = [jnp.exp(dmat[h] - m_row[h]) * s[h] for h in hs]
    gdec = [jnp.exp(inter[h] - m_row[h]) for h in hs]
    c_old = [c_ref[h] for h in hs]
    n_old = [n_ref[h] for h in hs]
    num = [jnp.dot(p[h].astype(BF16), v_b[h], preferred_element_type=F32)
           + gdec[h] * jnp.dot(q_b[h], c_old[h].astype(BF16), preferred_element_type=F32)
           for h in hs]
    yield next(units)

    hh = []
    for h in hs:
        den = (jnp.sum(p[h], axis=-1, keepdims=True)
               + gdec[h] * jnp.sum(q[h] * n_old[h], axis=-1, keepdims=True))
        hh.append(num[h] / jnp.maximum(jnp.abs(den), jnp.exp(-m_row[h])))
    for h in hs:
        b_last = b_col[h][L - 1:L, :]
        a_col = b_last - b_col[h] + li_col[h]
        m_new = jnp.maximum(b_last + m_prev[h], jnp.max(a_col, axis=0, keepdims=True))
        decay = jnp.exp(b_last + m_prev[h] - m_new)
        kw = k[h] * jnp.exp(a_col - m_new)
        c_ref[h] = decay * c_old[h] + jnp.dot(kw.T.astype(BF16), v_b[h],
                                              preferred_element_type=F32)
        n_ref[h] = decay * n_old[h] + jnp.sum(kw, axis=0, keepdims=True)
        m_ref[h] = jnp.broadcast_to(m_new, (1, LANES))
    yield next(units)

    for h in hs:
        hn = _rms(hh[h], mng_ref[:, h * DH:(h + 1) * DH])
        c0 = MIX_B + h * DH
        mix_ref[:, c0:c0 + DH] = _gate(hn, zc(Z_O + h * DH, DH)).astype(BF16)


def _mixer_layer_kernel(x_ref, xnext_ref, g_ref, win_ref, *rest, steps_per_seq):
    prm = rest[:12]
    wout_ref, o_ref = rest[12:14]
    za, zb, xna, xnb, mixa, mixb = rest[14:20]
    state = rest[20:]
    abuf, qkbuf, c_ref, n_ref, m_ref = state
    L = CHUNK
    s = pl.program_id(0)
    rows0 = lambda c0: (slice(0, L), slice(c0, c0 + Z_COL_BLOCK))
    rows1 = lambda c0: (slice(L, 2 * L), slice(c0, c0 + Z_COL_BLOCK))

    @pl.when(s == 0)
    def _():
        for task in _project_tasks(lambda: x_ref[0:L, :], g_ref, win_ref, xna, za):
            task()

    @pl.when(lax.rem(s, steps_per_seq) == 0)
    def _():
        abuf[0:CONV_HALO, :] = jnp.zeros((CONV_HALO, CONV_CH), F32)
        qkbuf[0:QK_HALO, :] = jnp.zeros((QK_HALO, 2 * MLSTM_CH), F32)
        c_ref[...] = jnp.zeros_like(c_ref)
        n_ref[...] = jnp.zeros_like(n_ref)
        m_ref[...] = jnp.zeros_like(m_ref)

    _interleave(_mix_stages(za, mixa, prm, state),
                _project_tasks(lambda: x_ref[L:2 * L, :], g_ref, win_ref, xnb, zb))
    _interleave(_mix_stages(zb, mixb, prm, state),
                _out_proj_tasks(lambda c0: x_ref.at[rows0(c0)], mixa, wout_ref,
                                lambda c0: o_ref.at[rows0(c0)])
                + _project_tasks(lambda: xnext_ref[...], g_ref, win_ref, xna, za))
    for task in _out_proj_tasks(lambda c0: x_ref.at[rows1(c0)], mixb, wout_ref,
                                lambda c0: o_ref.at[rows1(c0)]):
        task()


def _mixer_layer(x2, layer, p, big, *, seq):
    m, d = x2.shape
    t = 2 * CHUNK
    steps = m // t
    last_chunk = m // CHUNK - 1

    def stacked(a, idx):
        return pl.BlockSpec((None,) + a.shape[1:], (lambda s: (idx,) + (0,) * (a.ndim - 1)),
                            pipeline_mode=pl.Buffered(1))

    names = ["ln_mix_g", "w_in", "conv_w", "conv_b", "conv_norm_g", "conv_norm_b", "qk_conv_w",
             "qk_conv_b", "gate_bias", "mlstm_norm_g", "gm_norm_g", "gm_norm_b", "gm_w", "gm_bt",
             "w_out"]
    pairs = [big[k] if k in big else (p[k], layer) for k in names]
    args = [a for a, _ in pairs]
    return pl.pallas_call(
        functools.partial(_mixer_layer_kernel, steps_per_seq=seq // t),
        grid=(steps,),
        in_specs=[
            pl.BlockSpec((t, d), lambda s: (s, 0)),
            pl.BlockSpec((CHUNK, d), lambda s: (jnp.minimum(2 * s + 2, last_chunk), 0)),
            *[stacked(a, idx) for a, idx in pairs],
        ],
        out_specs=pl.BlockSpec((t, d), lambda s: (s, 0)),
        out_shape=jax.ShapeDtypeStruct((m, d), F32),
        scratch_shapes=[
            pltpu.VMEM((CHUNK, Z_COLS_PADDED), F32),
            pltpu.VMEM((CHUNK, Z_COLS_PADDED), F32),
            pltpu.VMEM((CHUNK, d), BF16),
            pltpu.VMEM((CHUNK, d), BF16),
            pltpu.VMEM((CHUNK, d), BF16),
            pltpu.VMEM((CHUNK, d), BF16),
            pltpu.VMEM((CONV_HALO + CHUNK, CONV_CH), F32),
            pltpu.VMEM((QK_HALO + CHUNK, 2 * MLSTM_CH), F32),
            pltpu.VMEM((HEADS, DH, DH), F32),
            pltpu.VMEM((HEADS, 1, DH), F32),
            pltpu.VMEM((HEADS, 1, LANES), F32),
        ],
        compiler_params=pltpu.CompilerParams(
            dimension_semantics=("arbitrary",),
            vmem_limit_bytes=MIXER_VMEM_LIMIT),
        name="mixer_layer",
    )(x2, x2, *args)


def _mlp_kernel(x_hbm, g_ref, wu_ref, wd_ref, fg_ref, *rest, final_norm, tm, n_cast):
    cast_in = rest[:n_cast]
    o_ref = rest[n_cast]
    cast_out = rest[n_cast + 1:2 * n_cast + 1]
    xbuf, xn_ref, sem = rest[2 * n_cast + 1:]
    i = pl.program_id(0)
    k = pl.program_id(1)

    for src, dst in zip(cast_in, cast_out):
        if dst.shape[1] != src.shape[1]:
            dst[...] = jnp.zeros_like(dst)
        dst[:, 0:src.shape[1]] = src[...].astype(BF16)

    def x_copy(tile):
        return pltpu.make_async_copy(x_hbm.at[pl.ds(tile * tm, tm), :], xbuf, sem)

    @pl.when(jnp.logical_and(i == 0, k == 0))
    def _():
        x_copy(0).start()

    @pl.when(k == 0)
    def _():
        x_copy(i).wait()
        x = xbuf[...]
        xn_ref[...] = _rms(x, g_ref[...]).astype(BF16)
        o_ref[...] = x

    @pl.when(jnp.logical_and(k == 1, i + 1 < pl.num_programs(0)))
    def _():
        x_copy(i + 1).start()

    hm = jnp.dot(xn_ref[...], wu_ref[...], preferred_element_type=F32)
    hm = jnp.maximum(hm, 0.0)
    o_ref[...] += jnp.dot((hm * hm).astype(BF16), wd_ref[...], preferred_element_type=F32)

    if final_norm:
        @pl.when(k == pl.num_programs(1) - 1)
        def _():
            o_ref[...] = _rms(o_ref[...], fg_ref[...])


def _mlp(x2, layer, g, wu, wd, fg, cast_next, *, final_norm, tm=1024, tf=1024):
    m, d = x2.shape
    (wu, wu_idx), (wd, wd_idx) = wu, wd
    f = wu.shape[2]
    nk = f // tf
    steps = (m // tm) * nk
    assert nk >= 2
    step_rows = lambda a: pl.BlockSpec((a.shape[0] // steps, a.shape[1]),
                                       lambda i, k: (i * nk + k, 0))
    next_rows = lambda a: pl.BlockSpec((None, a.shape[1] // steps, a.shape[2]),
                                       lambda i, k: (layer + 1, i * nk + k, 0))
    cast_shapes = [jax.ShapeDtypeStruct((a.shape[1], cols), BF16) for a, cols in cast_next]
    assert all(a.shape[1] % (2 * SUBLANES * steps) == 0 for a, _ in cast_next)
    outs = pl.pallas_call(
        functools.partial(_mlp_kernel, final_norm=final_norm, tm=tm, n_cast=len(cast_next)),
        grid=(m // tm, nk),
        in_specs=[
            pl.BlockSpec(memory_space=pl.ANY),
            pl.BlockSpec((None, 1, d), lambda i, k: (layer, 0, 0)),
            pl.BlockSpec((None, d, tf), lambda i, k: (wu_idx, 0, k)),
            pl.BlockSpec((None, tf, d), lambda i, k: (wd_idx, k, 0)),
            pl.BlockSpec((1, d), lambda i, k: (0, 0)),
            *[next_rows(a) for a, _ in cast_next],
        ],
        out_specs=[pl.BlockSpec((tm, d), lambda i, k: (i, 0)),
                   *[step_rows(s) for s in cast_shapes]],
        out_shape=[jax.ShapeDtypeStruct((m, d), F32), *cast_shapes],
        scratch_shapes=[pltpu.VMEM((tm, d), F32), pltpu.VMEM((tm, d), BF16),
                        pltpu.SemaphoreType.DMA(())],
        compiler_params=pltpu.CompilerParams(
            dimension_semantics=("arbitrary", "arbitrary"),
            vmem_limit_bytes=MLP_VMEM_LIMIT),
        name="mlp",
    )(x2, g, wu, wd, fg, *[a for a, _ in cast_next])
    return outs[0], outs[1:]


def kernel(x, ln_mix_g, w_in, conv_w, conv_b, conv_norm_g, conv_norm_b, qk_conv_w, qk_conv_b,
           igate_b, fgate_b, mlstm_norm_g, gm_norm_g, gm_norm_b, gm_w, gm_b, w_out,
           ln_mlp_g, w_up, w_down, final_g):
    batch, seq, d = x.shape
    depth = w_in.shape[0]
    x2 = x.reshape(batch * seq, d)
    rows = lambda a: a.reshape(depth, 1, -1)
    gate_bias = jnp.pad(jnp.concatenate([igate_b, fgate_b], axis=-1),
                        ((0, 0), (0, LANES - 2 * HEADS)))
    params = dict(
        ln_mix_g=rows(ln_mix_g), conv_w=conv_w, conv_b=rows(conv_b),
        conv_norm_g=rows(conv_norm_g), conv_norm_b=rows(conv_norm_b), qk_conv_w=qk_conv_w,
        qk_conv_b=rows(qk_conv_b), gate_bias=rows(gate_bias), mlstm_norm_g=rows(mlstm_norm_g),
        gm_norm_g=rows(gm_norm_g), gm_norm_b=rows(gm_norm_b), gm_w=gm_w,
        gm_bt=jnp.swapaxes(gm_b, 1, 2))
    big = dict(
        w_in=(jnp.concatenate([w_in[:1].astype(BF16),
                               jnp.zeros((1, d, Z_COLS_PADDED - Z_COLS), BF16)], axis=-1), 0),
        w_out=(w_out[:1].astype(BF16), 0),
        w_up=(w_up[:1].astype(BF16), 0),
        w_down=(w_down[:1].astype(BF16), 0))
    for l in range(depth):
        x2 = _mixer_layer(x2, l, params, big, seq=seq)
        last = l == depth - 1
        cast_next = [] if last else [(w_in, Z_COLS_PADDED), (w_out, d),
                                     (w_up, w_up.shape[2]), (w_down, d)]
        x2, casted = _mlp(x2, l, rows(ln_mlp_g), big["w_up"], big["w_down"],
                          final_g.reshape(1, -1), cast_next, final_norm=last)
        if not last:
            big = {name: (a[None], 0) for name, a in
                   zip(("w_in", "w_out", "w_up", "w_down"), casted)}
    return x2.reshape(batch, seq, d)
```

```python
import functools

import jax
import jax.numpy as jnp
from jax import lax
from jax.experimental import pallas as pl
from jax.experimental.pallas import tpu as pltpu

F32 = jnp.float32
BF16 = jnp.bfloat16

EPS = 1e-6
CONV_CH = 512
CONV_WIDTH = 31
HEADS = 4
DH = 256
MLSTM_CH = HEADS * DH
QK_WIDTH = 4
CHUNK = 128
GM_GROUPS = 4
GM_CH = 512
GM_GROUP_CH = GM_CH // GM_GROUPS
LANES = 128
SUBLANES = 8

Z_CV = 0
Z_CG = Z_CV + CONV_CH
Z_Q = Z_CG + CONV_CH
Z_K = Z_Q + MLSTM_CH
Z_V = Z_K + MLSTM_CH
Z_O = Z_V + MLSTM_CH
Z_GATE = Z_O + MLSTM_CH
Z_GU = Z_GATE + 2 * HEADS
Z_GV = Z_GU + GM_CH
Z_COLS = Z_GV + GM_CH
Z_COLS_PADDED = -(-Z_COLS // LANES) * LANES
Z_COL_BLOCK = 512
MIX_A = 0
MIX_B = CONV_CH
MIX_C = CONV_CH + MLSTM_CH

CONV_HALO = 32
QK_HALO = 8

MLP_VMEM_LIMIT = 56 * 1024 * 1024
MIXER_VMEM_LIMIT = 60 * 1024 * 1024


def _rms(x, g):
    return x * lax.rsqrt(jnp.mean(x * x, axis=-1, keepdims=True) + EPS) * g


def _layernorm(x, g, b):
    mu = jnp.mean(x, axis=-1, keepdims=True)
    xc = x - mu
    var = jnp.mean(xc * xc, axis=-1, keepdims=True)
    return xc * lax.rsqrt(var + EPS) * g + b


def _gate(v, x):
    hv = 0.5 * v
    return hv * jnp.tanh(0.5 * x) + hv


def _silu(x):
    h = 0.5 * x
    return h * jnp.tanh(h) + h


def _cumsum_rows(x):
    rows = lax.broadcasted_iota(jnp.int32, x.shape, 0)
    shift = 1
    while shift < x.shape[0]:
        x = x + jnp.where(rows >= shift, pltpu.roll(x, shift, axis=0), 0.0)
        shift *= 2
    return x


def _shift_up(x, r):
    return x if r == 0 else pltpu.roll(x, x.shape[0] - r, axis=0)


def _causal_conv(buf, w_ref, b_ref, cols, *, width, halo):
    L = CHUNK
    x = buf[:, cols]
    acc = jnp.broadcast_to(b_ref[:, cols], (L, LANES))
    first = halo - (width - 1)
    for r in range(SUBLANES):
        starts = [s for s in range(first, halo + 1) if s % SUBLANES == r]
        if not starts:
            continue
        xs = _shift_up(x, r)
        for s in starts:
            k = s - first
            a0 = s - r
            acc = acc + w_ref[k:k + 1, cols] * xs[a0:a0 + L]
    return acc


def _short_conv_block(buf, w_ref, b_ref, cols):
    L = CHUNK
    n = QK_HALO + L
    x = buf[:, cols]
    x1 = pltpu.roll(x, 1, axis=0)
    w = lambda k: w_ref[k:k + 1, cols]
    u = w(3) * x + w(2) * x1
    v = w(1) * x + w(0) * x1
    y = u + pltpu.roll(v, 2, axis=0)
    assert QK_WIDTH == 4 and QK_HALO >= 3 and n % SUBLANES == 0
    return b_ref[:, cols] + y[QK_HALO:n]


def _project_tasks(x_rows, g_ref, win_ref, xn_ref, z_ref):
    def norm():
        xn_ref[...] = _rms(x_rows(), g_ref[...]).astype(BF16)

    def block(c0, c1):
        def run():
            z_ref[:, c0:c1] = jnp.dot(xn_ref[...], win_ref[:, c0:c1], preferred_element_type=F32)
        return run

    starts = list(range(0, Z_GATE, Z_COL_BLOCK))
    return [norm] + [block(c0, c0 + Z_COL_BLOCK) for c0 in starts] + [block(Z_GATE, Z_COLS)]


def _out_proj_tasks(x_rows, mix_ref, wout_ref, o_rows):
    def block(c0):
        def run():
            o_rows(c0)[...] = x_rows(c0)[...] + jnp.dot(
                mix_ref[...], wout_ref[:, c0:c0 + Z_COL_BLOCK], preferred_element_type=F32)
        return run
    return [block(c0) for c0 in range(0, wout_ref.shape[1], Z_COL_BLOCK)]


def _interleave(stages, tasks):
    total = sum(MIX_STAGE_UNITS)
    done = 0.0
    issued = 0
    for cost in stages:
        done += cost
        target = round(len(tasks) * done / total)
        while issued < target:
            tasks[issued]()
            issued += 1
    for t in tasks[issued:]:
        t()


MIX_STAGE_UNITS = (5.0, 4.5, 1.5, 2.0, 1.5)


def _mix_stages(z_ref, mix_ref, prm, state):
    units = iter(MIX_STAGE_UNITS)
    yield next(units)
    (convw_ref, convb_ref, cng_ref, cnb_ref, qkw_ref, qkb_ref, gbias_ref, mng_ref,
     gng_ref, gnb_ref, gmw_ref, gmbt_ref) = prm
    abuf, qkbuf, c_ref, n_ref, m_ref = state
    L = CHUNK
    rows = lax.broadcasted_iota(jnp.int32, (L, L), 0)
    cols = lax.broadcasted_iota(jnp.int32, (L, L), 1)
    causal = cols <= rows
    lanes = lax.broadcasted_iota(jnp.int32, (L, LANES), 1)
    zc = lambda c0, n: z_ref[:, c0:c0 + n]

    abuf[CONV_HALO:CONV_HALO + L, :] = _gate(zc(Z_CV, CONV_CH), zc(Z_CG, CONV_CH))
    acc = jnp.concatenate(
        [_causal_conv(abuf, convw_ref, convb_ref, slice(c * LANES, (c + 1) * LANES),
                      width=CONV_WIDTH, halo=CONV_HALO) for c in range(CONV_CH // LANES)],
        axis=-1)
    abuf[0:CONV_HALO, :] = abuf[L:L + CONV_HALO, :]
    a = _layernorm(acc, cng_ref[...], cnb_ref[...])
    mix_ref[:, MIX_A:MIX_A + CONV_CH] = _silu(a).astype(BF16)

    gu = jax.nn.gelu(zc(Z_GU, GM_CH))
    gv = _layernorm(jax.nn.gelu(zc(Z_GV, GM_CH)), gng_ref[...], gnb_ref[...])
    for g in range(GM_GROUPS):
        sl = slice(g * GM_GROUP_CH, (g + 1) * GM_GROUP_CH)
        wc = jnp.where(causal, gmw_ref[g], 0.0).astype(BF16)
        sp = jnp.dot(wc, gv[:, sl].astype(BF16), preferred_element_type=F32)
        sp = sp + gmbt_ref[:, g:g + 1]
        c0 = MIX_C + g * GM_GROUP_CH
        mix_ref[:, c0:c0 + GM_GROUP_CH] = (gu[:, sl] * sp).astype(BF16)
    yield next(units)

    pre = zc(Z_GATE, LANES) + gbias_ref[...]
    gl = jnp.where(lanes < HEADS, pre, jax.nn.log_sigmoid(pre))
    bc = _cumsum_rows(gl)
    gl_t = gl.T
    bc_t = bc.T

    qkbuf[QK_HALO:QK_HALO + L, :] = zc(Z_Q, 2 * MLSTM_CH)

    def short_conv(col):
        y = jnp.concatenate(
            [_short_conv_block(qkbuf, qkw_ref, qkb_ref, slice(col + c * LANES, col + (c + 1) * LANES))
             for c in range(DH // LANES)], axis=-1)
        return _silu(y)

    hs = range(HEADS)
    q = [short_conv(h * DH) for h in hs]
    k = [short_conv(MLSTM_CH + h * DH) * (DH ** -0.5) for h in hs]
    qkbuf[0:QK_HALO, :] = qkbuf[L:L + QK_HALO, :]
    q_b = [q[h].astype(BF16) for h in hs]
    v_b = [zc(Z_V + h * DH, DH).astype(BF16) for h in hs]
    li_col = [gl[:, h:h + 1] for h in hs]
    b_col = [bc[:, HEADS + h:HEADS + h + 1] for h in hs]
    m_prev = [m_ref[h][:, 0:1] for h in hs]
    dmat, inter, m_row = [], [], []
    for h in hs:
        li_row = gl_t[h:h + 1, :]
        b_row = bc_t[HEADS + h:HEADS + h + 1, :]
        dmat.append(jnp.where(causal, b_col[h] - b_row + li_row, -jnp.inf))
        inter.append(b_col[h] + m_prev[h])
        m_row.append(jnp.maximum(jnp.max(dmat[h], axis=-1, keepdims=True), inter[h]))
    s = [lax.dot_general(q_b[h], k[h].astype(BF16), (((1,), (1,)), ((), ())),
                         preferred_element_type=F32) for h in hs]
    yield next(units)

    p = [jnp.exp(dmat[h] - m_row[h]) * s[h] for h in hs]
    gdec = [jnp.exp(inter[h] - m_row[h]) for h in hs]
    c_old = [c_ref[h] for h in hs]
    n_old = [n_ref[h] for h in hs]
    num = [jnp.dot(p[h].astype(BF16), v_b[h], preferred_element_type=F32)
           + gdec[h] * jnp.dot(q_b[h], c_old[h].astype(BF16), preferred_element_type=F32)
           for h in hs]
    yield next(units)

    hh = []
    for h in hs:
        den = (jnp.sum(p[h], axis=-1, keepdims=True)
               + gdec[h] * jnp.sum(q[h] * n_old[h], axis=-1, keepdims=True))
        hh.append(num[h] / jnp.maximum(jnp.abs(den), jnp.exp(-m_row[h])))
    for h in hs:
        b_last = b_col[h][L - 1:L, :]
        a_col = b_last - b_col[h] + li_col[h]
        m_new = jnp.maximum(b_last + m_prev[h], jnp.max(a_col, axis=0, keepdims=True))
        decay = jnp.exp(b_last + m_prev[h] - m_new)
        kw = k[h] * jnp.exp(a_col - m_new)
        c_ref[h] = decay * c_old[h] + jnp.dot(kw.T.astype(BF16), v_b[h],
                                              preferred_element_type=F32)
        n_ref[h] = decay * n_old[h] + jnp.sum(kw, axis=0, keepdims=True)
        m_ref[h] = jnp.broadcast_to(m_new, (1, LANES))
    yield next(units)

    for h in hs:
        hn = _rms(hh[h], mng_ref[:, h * DH:(h + 1) * DH])
        c0 = MIX_B + h * DH
        mix_ref[:, c0:c0 + DH] = _gate(hn, zc(Z_O + h * DH, DH)).astype(BF16)


def _mixer_layer_kernel(x_ref, xnext_ref, g_ref, win_ref, *rest, steps_per_seq):
    prm = rest[:12]
    wout_ref, o_ref = rest[12:14]
    za, zb, xna, xnb, mixa, mixb = rest[14:20]
    state = rest[20:]
    abuf, qkbuf, c_ref, n_ref, m_ref = state
    L = CHUNK
    s = pl.program_id(0)
    rows0 = lambda c0: (slice(0, L), slice(c0, c0 + Z_COL_BLOCK))
    rows1 = lambda c0: (slice(L, 2 * L), slice(c0, c0 + Z_COL_BLOCK))

    @pl.when(s == 0)
    def _():
        for task in _project_tasks(lambda: x_ref[0:L, :], g_ref, win_ref, xna, za):
            task()

    @pl.when(lax.rem(s, steps_per_seq) == 0)
    def _():
        abuf[0:CONV_HALO, :] = jnp.zeros((CONV_HALO, CONV_CH), F32)
        qkbuf[0:QK_HALO, :] = jnp.zeros((QK_HALO, 2 * MLSTM_CH), F32)
        c_ref[...] = jnp.zeros_like(c_ref)
        n_ref[...] = jnp.zeros_like(n_ref)
        m_ref[...] = jnp.zeros_like(m_ref)

    _interleave(_mix_stages(za, mixa, prm, state),
                _project_tasks(lambda: x_ref[L:2 * L, :], g_ref, win_ref, xnb, zb))
    _interleave(_mix_stages(zb, mixb, prm, state),
                _out_proj_tasks(lambda c0: x_ref.at[rows0(c0)], mixa, wout_ref,
                                lambda c0: o_ref.at[rows0(c0)])
                + _project_tasks(lambda: xnext_ref[...], g_ref, win_ref, xna, za))
    for task in _out_proj_tasks(lambda c0: x_ref.at[rows1(c0)], mixb, wout_ref,
                                lambda c0: o_ref.at[rows1(c0)]):
        task()


def _mixer_layer(x2, layer, p, big, *, seq):
    m, d = x2.shape
    t = 2 * CHUNK
    steps = m // t
    last_chunk = m // CHUNK - 1

    def stacked(a, idx):
        return pl.BlockSpec((None,) + a.shape[1:], (lambda s: (idx,) + (0,) * (a.ndim - 1)),
                            pipeline_mode=pl.Buffered(1))

    names = ["ln_mix_g", "w_in", "conv_w", "conv_b", "conv_norm_g", "conv_norm_b", "qk_conv_w",
             "qk_conv_b", "gate_bias", "mlstm_norm_g", "gm_norm_g", "gm_norm_b", "gm_w", "gm_bt",
             "w_out"]
    pairs = [big[k] if k in big else (p[k], layer) for k in names]
    args = [a for a, _ in pairs]
    return pl.pallas_call(
        functools.partial(_mixer_layer_kernel, steps_per_seq=seq // t),
        grid=(steps,),
        in_specs=[
            pl.BlockSpec((t, d), lambda s: (s, 0)),
            pl.BlockSpec((CHUNK, d), lambda s: (jnp.minimum(2 * s + 2, last_chunk), 0)),
            *[stacked(a, idx) for a, idx in pairs],
        ],
        out_specs=pl.BlockSpec((t, d), lambda s: (s, 0)),
        out_shape=jax.ShapeDtypeStruct((m, d), F32),
        scratch_shapes=[
            pltpu.VMEM((CHUNK, Z_COLS_PADDED), F32),
            pltpu.VMEM((CHUNK, Z_COLS_PADDED), F32),
            pltpu.VMEM((CHUNK, d), BF16),
            pltpu.VMEM((CHUNK, d), BF16),
            pltpu.VMEM((CHUNK, d), BF16),
            pltpu.VMEM((CHUNK, d), BF16),
            pltpu.VMEM((CONV_HALO + CHUNK, CONV_CH), F32),
            pltpu.VMEM((QK_HALO + CHUNK, 2 * MLSTM_CH), F32),
            pltpu.VMEM((HEADS, DH, DH), F32),
            pltpu.VMEM((HEADS, 1, DH), F32),
            pltpu.VMEM((HEADS, 1, LANES), F32),
        ],
        compiler_params=pltpu.CompilerParams(
            dimension_semantics=("arbitrary",),
            vmem_limit_bytes=MIXER_VMEM_LIMIT),
        name="mixer_layer",
    )(x2, x2, *args)


def _cast_blocks(srcs, dsts):
    for src, dst in zip(srcs, dsts):
        if dst.shape[1] != src.shape[1]:
            dst[...] = jnp.zeros_like(dst)
        dst[:, 0:src.shape[1]] = src[...].astype(BF16)


def _cast_kernel(*refs):
    _cast_blocks(refs[:len(refs) // 2], refs[len(refs) // 2:])


def _cast_layer(weights, layer, *, steps=32):
    shapes = [jax.ShapeDtypeStruct((a.shape[1], cols), BF16) for a, cols in weights]
    assert all(a.shape[1] % (2 * SUBLANES * steps) == 0 for a, _ in weights)
    return pl.pallas_call(
        _cast_kernel,
        grid=(steps,),
        in_specs=[pl.BlockSpec((None, a.shape[1] // steps, a.shape[2]), lambda s: (layer, s, 0))
                  for a, _ in weights],
        out_specs=[pl.BlockSpec((sh.shape[0] // steps, sh.shape[1]), lambda s: (s, 0))
                   for sh in shapes],
        out_shape=shapes,
        compiler_params=pltpu.CompilerParams(
            dimension_semantics=("arbitrary",), vmem_limit_bytes=MLP_VMEM_LIMIT),
        name="cast_weights",
    )(*[a for a, _ in weights])


def _mlp_kernel(x_hbm, g_ref, wu_ref, wd_ref, fg_ref, *rest, final_norm, tm, n_cast):
    cast_in = rest[:n_cast]
    o_ref = rest[n_cast]
    cast_out = rest[n_cast + 1:2 * n_cast + 1]
    xbuf, xn_ref, sem = rest[2 * n_cast + 1:]
    i = pl.program_id(0)
    k = pl.program_id(1)

    _cast_blocks(cast_in, cast_out)

    def x_copy(tile):
        return pltpu.make_async_copy(x_hbm.at[pl.ds(tile * tm, tm), :], xbuf, sem)

    @pl.when(jnp.logical_and(i == 0, k == 0))
    def _():
        x_copy(0).start()

    @pl.when(k == 0)
    def _():
        x_copy(i).wait()
        x = xbuf[...]
        xn_ref[...] = _rms(x, g_ref[...]).astype(BF16)
        o_ref[...] = x

    @pl.when(jnp.logical_and(k == 1, i + 1 < pl.num_programs(0)))
    def _():
        x_copy(i + 1).start()

    hm = jnp.dot(xn_ref[...], wu_ref[...], preferred_element_type=F32)
    hm = jnp.maximum(hm, 0.0)
    o_ref[...] += jnp.dot((hm * hm).astype(BF16), wd_ref[...], preferred_element_type=F32)

    if final_norm:
        @pl.when(k == pl.num_programs(1) - 1)
        def _():
            o_ref[...] = _rms(o_ref[...], fg_ref[...])


def _mlp(x2, layer, g, wu, wd, fg, cast_next, *, final_norm, tm=1024, tf=1024):
    m, d = x2.shape
    (wu, wu_idx), (wd, wd_idx) = wu, wd
    f = wu.shape[2]
    nk = f // tf
    steps = (m // tm) * nk
    assert nk >= 2
    step_rows = lambda a: pl.BlockSpec((a.shape[0] // steps, a.shape[1]),
                                       lambda i, k: (i * nk + k, 0))
    next_rows = lambda a: pl.BlockSpec((None, a.shape[1] // steps, a.shape[2]),
                                       lambda i, k: (layer + 1, i * nk + k, 0))
    cast_shapes = [jax.ShapeDtypeStruct((a.shape[1], cols), BF16) for a, cols in cast_next]
    assert all(a.shape[1] % (2 * SUBLANES * steps) == 0 for a, _ in cast_next)
    outs = pl.pallas_call(
        functools.partial(_mlp_kernel, final_norm=final_norm, tm=tm, n_cast=len(cast_next)),
        grid=(m // tm, nk),
        in_specs=[
            pl.BlockSpec(memory_space=pl.ANY),
            pl.BlockSpec((None, 1, d), lambda i, k: (layer, 0, 0)),
            pl.BlockSpec((None, d, tf), lambda i, k: (wu_idx, 0, k)),
            pl.BlockSpec((None, tf, d), lambda i, k: (wd_idx, k, 0)),
            pl.BlockSpec((1, d), lambda i, k: (0, 0)),
            *[next_rows(a) for a, _ in cast_next],
        ],
        out_specs=[pl.BlockSpec((tm, d), lambda i, k: (i, 0)),
                   *[step_rows(s) for s in cast_shapes]],
        out_shape=[jax.ShapeDtypeStruct((m, d), F32), *cast_shapes],
        scratch_shapes=[pltpu.VMEM((tm, d), F32), pltpu.VMEM((tm, d), BF16),
                        pltpu.SemaphoreType.DMA(())],
        compiler_params=pltpu.CompilerParams(
            dimension_semantics=("arbitrary", "arbitrary"),
            vmem_limit_bytes=MLP_VMEM_LIMIT),
        name="mlp",
    )(x2, g, wu, wd, fg, *[a for a, _ in cast_next])
    return outs[0], outs[1:]


def kernel(x, ln_mix_g, w_in, conv_w, conv_b, conv_norm_g, conv_norm_b, qk_conv_w, qk_conv_b,
           igate_b, fgate_b, mlstm_norm_g, gm_norm_g, gm_norm_b, gm_w, gm_b, w_out,
           ln_mlp_g, w_up, w_down, final_g):
    batch, seq, d = x.shape
    depth = w_in.shape[0]
    x2 = x.reshape(batch * seq, d)
    rows = lambda a: a.reshape(depth, 1, -1)
    gate_bias = jnp.pad(jnp.concatenate([igate_b, fgate_b], axis=-1),
                        ((0, 0), (0, LANES - 2 * HEADS)))
    params = dict(
        ln_mix_g=rows(ln_mix_g), conv_w=conv_w, conv_b=rows(conv_b),
        conv_norm_g=rows(conv_norm_g), conv_norm_b=rows(conv_norm_b), qk_conv_w=qk_conv_w,
        qk_conv_b=rows(qk_conv_b), gate_bias=rows(gate_bias), mlstm_norm_g=rows(mlstm_norm_g),
        gm_norm_g=rows(gm_norm_g), gm_norm_b=rows(gm_norm_b), gm_w=gm_w,
        gm_bt=jnp.swapaxes(gm_b, 1, 2))
    big_names = ("w_in", "w_out", "w_up", "w_down")
    weights = [(w_in, Z_COLS_PADDED), (w_out, d), (w_up, w_up.shape[2]), (w_down, d)]
    as_big = lambda casted: {name: (a[None], 0) for name, a in zip(big_names, casted)}
    big = as_big(_cast_layer(weights, 0))
    for l in range(depth):
        x2 = _mixer_layer(x2, l, params, big, seq=seq)
        last = l == depth - 1
        x2, casted = _mlp(x2, l, rows(ln_mlp_g), big["w_up"], big["w_down"],
                          final_g.reshape(1, -1), [] if last else weights, final_norm=last)
        if not last:
            big = as_big(casted)
    return x2.reshape(batch, seq, d)
```
